```python
import jax, jax.numpy as jnp
from jax import lax
import numpy as np

D_MODEL = 1024
BATCH = 16
SEQ = 4096
DEPTH = 2

N_MIXERS = 2
PLE_DIM = 256
D_FF = 4 * D_MODEL
EPS = 1e-6

HG_KDIM = 128
HG_HEADS = D_MODEL // HG_KDIM
HG_VDIM = D_MODEL // HG_HEADS
HG_QK = HG_HEADS * HG_KDIM
HG_V = HG_HEADS * HG_VDIM
HG_CHUNK = 64
N_HGRN = (DEPTH + 1) // 2

MLA_HEADS = 16
MLA_Q_RANK = 384
MLA_KV_RANK = 256
MLA_NOPE = 128
MLA_ROPE = 64
MLA_V = 128
ROPE_BASE = 10000.0
Q_BLOCK = 128
N_MLA = DEPTH // 2

kernel_name = "hybrid_hgrn2_mla_encoder"


def rms_norm(x, gain):
    xf = x.astype(jnp.float32)
    y = xf * lax.rsqrt(jnp.mean(xf * xf, axis=-1, keepdims=True) + EPS)
    return (y * gain.astype(jnp.float32)).astype(x.dtype)


def gla_chunk_scan(q, k, v, g):
    b, h, s, dk = q.shape
    dv = v.shape[-1]
    nc = s // HG_CHUNK

    def to_chunks(t):
        return jnp.moveaxis(t.reshape(b, h, nc, HG_CHUNK, t.shape[-1]), 2, 0)

    qc, kc, vc, gc = to_chunks(q), to_chunks(k), to_chunks(v), to_chunks(g)
    mask = jnp.tril(jnp.ones((HG_CHUNK, HG_CHUNK), dtype=bool))

    def step(state, inp):
        qi, ki, vi, gi = inp
        G = jnp.cumsum(gi.astype(jnp.float32), axis=-2)
        G_last = G[..., -1:, :]
        q_dec = qi * jnp.exp(G)
        k_inv = ki * jnp.exp(-G)
        k_end = ki * jnp.exp(G_last - G)
        attn = jnp.where(mask, jnp.einsum('bhik,bhjk->bhij', q_dec, k_inv), 0.0)
        o = (jnp.einsum('bhij,bhjv->bhiv', attn, vi)
             + jnp.einsum('bhik,bhkv->bhiv', q_dec, state))
        new_state = (jnp.exp(G_last[..., 0, :])[..., None] * state
                     + jnp.einsum('bhjk,bhjv->bhkv', k_end, vi))
        return new_state, o

    state0 = jnp.zeros((b, h, dk, dv), jnp.float32)
    _, o = lax.scan(step, state0, (qc, kc, vc, gc))
    return jnp.moveaxis(o, 0, 2).reshape(b, h, s, dv).astype(v.dtype)


def hgrn2_mixer(x, w_in, o_norm, w_out, lb):
    b, s, _ = x.shape
    proj = x @ w_in
    q, f_fw, f_bw, inp, gate = jnp.split(
        proj, [HG_QK, 2 * HG_QK, 3 * HG_QK, 3 * HG_QK + HG_V], axis=-1)

    def heads(t):
        return jnp.transpose(t.reshape(b, s, HG_HEADS, -1), (0, 2, 1, 3))

    q = heads(q) * (HG_KDIM ** -0.5)
    v = heads(inp)
    lbh = lb.astype(jnp.float32).reshape(HG_HEADS, 1, HG_KDIM)

    def gates(f_logit):
        f = lbh + (1.0 - lbh) * jax.nn.sigmoid(heads(f_logit).astype(jnp.float32))
        return 1.0 - f, jnp.log(f)

    k_f, g_f = gates(f_fw)
    k_b, g_b = gates(f_bw)
    flip = lambda t: jnp.flip(t, axis=2)
    o_f = gla_chunk_scan(q, k_f, v, g_f)
    o_b = flip(gla_chunk_scan(flip(q), flip(k_b), flip(v), flip(g_b)))
    o = rms_norm(o_f + o_b, o_norm[:, None, :])
    o = jnp.transpose(o, (0, 2, 1, 3)).reshape(b, s, HG_V)
    return (o * jax.nn.silu(gate)) @ w_out


def rope_angles(positions, dim):
    inv = 1.0 / (ROPE_BASE ** (jnp.arange(0, dim, 2, dtype=jnp.float32) / dim))
    ang = positions.astype(jnp.float32)[..., None] * inv
    return jnp.cos(ang), jnp.sin(ang)


def apply_rope(t, cos, sin):
    t1, t2 = jnp.split(t, 2, axis=-1)
    cos = cos.astype(t.dtype)
    sin = sin.astype(t.dtype)
    return jnp.concatenate([t1 * cos - t2 * sin, t1 * sin + t2 * cos], axis=-1)


def mla_mixer(x, positions, w_in, q_norm, w_uq, kv_norm, w_ukv, w_o):
    b, s, _ = x.shape
    proj = x @ w_in
    cq, ckv, k_rope = jnp.split(proj, [MLA_Q_RANK, MLA_Q_RANK + MLA_KV_RANK], axis=-1)
    cq = rms_norm(cq, q_norm)
    ckv = rms_norm(ckv, kv_norm)
    q = (cq @ w_uq).reshape(b, s, MLA_HEADS, MLA_NOPE + MLA_ROPE)
    q_nope, q_rope = q[..., :MLA_NOPE], q[..., MLA_NOPE:]
    cos, sin = rope_angles(positions, MLA_ROPE)
    q_rope = apply_rope(q_rope, cos[:, :, None, :], sin[:, :, None, :])
    k_rope = apply_rope(k_rope, cos, sin)
    w_ukv_h = w_ukv.reshape(MLA_KV_RANK, MLA_HEADS, MLA_NOPE + MLA_V)
    w_uk, w_uv = w_ukv_h[..., :MLA_NOPE], w_ukv_h[..., MLA_NOPE:]
    q_lat = jnp.einsum('bshn,chn->bshc', q_nope, w_uk)
    scale = (MLA_NOPE + MLA_ROPE) ** -0.5
    nb = s // Q_BLOCK

    def blockify(t):
        return jnp.moveaxis(t.reshape(b, nb, Q_BLOCK, *t.shape[2:]), 1, 0)

    def attend(blk):
        ql, qr = blk
        scores = (jnp.einsum('bqhc,bkc->bhqk', ql, ckv)
                  + jnp.einsum('bqhr,bkr->bhqk', qr, k_rope)) * scale
        probs = jax.nn.softmax(scores.astype(jnp.float32), axis=-1).astype(ckv.dtype)
        return jnp.einsum('bhqk,bkc->bqhc', probs, ckv)

    o_lat = lax.map(attend, (blockify(q_lat), blockify(q_rope)))
    o_lat = jnp.moveaxis(o_lat, 0, 1).reshape(b, s, MLA_HEADS, MLA_KV_RANK)
    o = jnp.einsum('bshc,chv->bshv', o_lat, w_uv).reshape(b, s, MLA_HEADS * MLA_V)
    return o @ w_o


def squared_relu_mlp(x, w1, w2):
    return jnp.square(jax.nn.relu(x @ w1)) @ w2


def setup_inputs(seed: int = 0) -> dict:
    key = jax.random.key(seed)
    ks = jax.random.split(key, 32)
    f32 = jnp.float32

    def nrm(k, shape, fan_in):
        return jax.random.normal(k, shape, f32) * (fan_in ** -0.5)

    def gain(k, shape):
        return 1.0 + 0.05 * jax.random.normal(k, shape, f32)

    x = jax.random.normal(ks[0], (BATCH, SEQ, D_MODEL), f32)
    p = jax.random.normal(ks[1], (DEPTH, BATCH, SEQ, PLE_DIM), f32)
    positions = (jnp.arange(SEQ, dtype=jnp.int32)[None, :]
                 + jax.random.randint(ks[2], (BATCH, 1), 0, SEQ, dtype=jnp.int32))
    return {
        "x": x,
        "p": p,
        "positions": positions,
        "pre_mix_norm": gain(ks[3], (DEPTH, D_MODEL)),
        "post_mix_norm": gain(ks[4], (DEPTH, D_MODEL)),
        "pre_mlp_norm": gain(ks[5], (DEPTH, D_MODEL)),
        "post_mlp_norm": gain(ks[6], (DEPTH, D_MODEL)),
        "w_mlp_in": nrm(ks[7], (DEPTH, D_MODEL, D_FF), D_MODEL),
        "w_mlp_out": nrm(ks[8], (DEPTH, D_FF, D_MODEL), D_FF),
        "w_ple_proj": nrm(ks[9], (DEPTH, PLE_DIM, D_MODEL), PLE_DIM),
        "w_ple_gate": nrm(ks[10], (DEPTH, D_MODEL, D_MODEL), D_MODEL),
        "ple_norm": gain(ks[11], (DEPTH, D_MODEL)),
        "hg_lb_logits": 0.1 * jax.random.normal(ks[12], (DEPTH + 1, HG_QK), f32),
        "hg_w_in": nrm(ks[13], (N_HGRN, D_MODEL, 3 * HG_QK + 2 * HG_V), D_MODEL),
        "hg_o_norm": gain(ks[14], (N_HGRN, HG_HEADS, HG_VDIM)),
        "hg_w_out": nrm(ks[15], (N_HGRN, HG_V, D_MODEL), HG_V),
        "mla_w_in": nrm(ks[16], (N_MLA, D_MODEL, MLA_Q_RANK + MLA_KV_RANK + MLA_ROPE), D_MODEL),
        "mla_q_norm": gain(ks[17], (N_MLA, MLA_Q_RANK)),
        "mla_w_uq": nrm(ks[18], (N_MLA, MLA_Q_RANK, MLA_HEADS * (MLA_NOPE + MLA_ROPE)), MLA_Q_RANK),
        "mla_kv_norm": gain(ks[19], (N_MLA, MLA_KV_RANK)),
        "mla_w_ukv": nrm(ks[20], (N_MLA, MLA_KV_RANK, MLA_HEADS * (MLA_NOPE + MLA_V)), MLA_KV_RANK),
        "mla_w_o": nrm(ks[21], (N_MLA, MLA_HEADS * MLA_V, D_MODEL), MLA_HEADS * MLA_V),
    }


def reference(x, p, positions, pre_mix_norm, post_mix_norm, pre_mlp_norm, post_mlp_norm,
              w_mlp_in, w_mlp_out, w_ple_proj, w_ple_gate, ple_norm,
              hg_lb_logits, hg_w_in, hg_o_norm, hg_w_out,
              mla_w_in, mla_q_norm, mla_w_uq, mla_kv_norm, mla_w_ukv, mla_w_o):
    lb_all = jnp.cumsum(jax.nn.softmax(hg_lb_logits.astype(jnp.float32), axis=0), axis=0)
    h = x
    for i in range(DEPTH):
        a = rms_norm(h, pre_mix_norm[i])
        j = i // N_MIXERS
        if i % N_MIXERS == 0:
            m = hgrn2_mixer(a, hg_w_in[j], hg_o_norm[j], hg_w_out[j], lb_all[i])
        else:
            m = mla_mixer(a, positions, mla_w_in[j], mla_q_norm[j], mla_w_uq[j],
                          mla_kv_norm[j], mla_w_ukv[j], mla_w_o[j])
        h = h + rms_norm(m, post_mix_norm[i])
        f = squared_relu_mlp(rms_norm(h, pre_mlp_norm[i]), w_mlp_in[i], w_mlp_out[i])
        h = h + rms_norm(f, post_mlp_norm[i])
        e = p[i].astype(h.dtype) @ w_ple_proj[i]
        g = jax.nn.sigmoid(h @ w_ple_gate[i])
        h = h + rms_norm(g * e, ple_norm[i])
    return h
```

```python
import functools
import math

import jax
import jax.numpy as jnp
from jax import lax
from jax.experimental import pallas as pl
from jax.experimental.pallas import tpu as pltpu

EPS = 1e-6
ROPE_BASE = 10000.0
LANE = 128
GLA_CHUNK = 64
VMEM_LIMIT = 56 * 1024 * 1024

F32 = jnp.float32
BF16 = jnp.bfloat16


def _rms(x, gain):
    ms = jnp.mean(x * x, axis=-1, keepdims=True)
    return x * lax.rsqrt(ms + EPS) * gain


def _params(**kw):
    return pltpu.CompilerParams(vmem_limit_bytes=VMEM_LIMIT, **kw)


def _const_spec(shape):
    nd = len(shape)
    return pl.BlockSpec(shape, lambda *_: (0,) * nd, pipeline_mode=pl.Buffered(1))


def _norm_matmul_kernel(x_ref, g_ref, w_ref, o_ref, xn_ref):
    @pl.when(pl.program_id(1) == 0)
    def _():
        xn_ref[...] = _rms(x_ref[...], g_ref[...]).astype(BF16)

    o_ref[...] = jnp.dot(xn_ref[...], w_ref[...],
                         preferred_element_type=F32).astype(o_ref.dtype)


def norm_matmul(x, gain, w, *, tm=1024, tn=1024):
    t, d = x.shape
    n = w.shape[1]
    return pl.pallas_call(
        _norm_matmul_kernel,
        grid=(t // tm, n // tn),
        in_specs=[
            pl.BlockSpec((tm, d), lambda i, j: (i, 0)),
            pl.BlockSpec((1, d), lambda i, j: (0, 0)),
            pl.BlockSpec((d, tn), lambda i, j: (0, j)),
        ],
        out_specs=pl.BlockSpec((tm, tn), lambda i, j: (i, j)),
        out_shape=jax.ShapeDtypeStruct((t, n), BF16),
        scratch_shapes=[pltpu.VMEM((tm, d), BF16)],
        compiler_params=_params(dimension_semantics=("arbitrary", "arbitrary")),
        name="norm_matmul",
    )(x, gain, w)


def _gla_kernel(lbl_ref, qf_ref, zf_ref, vf_ref, qb_ref, zb_ref, vb_ref,
                of_ref, ob_ref, st_ref, *, layer, heads, kdim, nsub):
    c = GLA_CHUNK

    @pl.when(pl.program_id(1) == 0)
    def _():
        st_ref[...] = jnp.zeros_like(st_ref)

    lg = lbl_ref[...].astype(F32)
    e = jnp.exp(lg - jnp.max(lg, axis=0, keepdims=True))
    lb = (jnp.sum(e[:layer + 1], axis=0, keepdims=True)
          / jnp.sum(e, axis=0, keepdims=True))

    row = lax.broadcasted_iota(jnp.int32, (c, c), 0)
    col = lax.broadcasted_iota(jnp.int32, (c, c), 1)
    scale = kdim ** -0.5

    def chunk(direction, q_ref, z_ref, v_ref, o_ref, r0):
        keep = (col <= row) if direction == 0 else (col >= row)
        tri = keep.astype(BF16)
        rows = pl.ds(r0, c)
        z = z_ref[rows, :].astype(F32)
        f = lb + (1.0 - lb) * jax.nn.sigmoid(z)
        k = 1.0 - f
        g = jnp.log(f)
        g_hi = g.astype(BF16)
        g_lo = (g - g_hi.astype(F32)).astype(BF16)
        gc = (jnp.dot(tri, g_hi, preferred_element_type=F32)
              + jnp.dot(tri, g_lo, preferred_element_type=F32))
        g_end = gc[c - 1:c, :] if direction == 0 else gc[0:1, :]
        q_dec = (q_ref[rows, :].astype(F32) * scale * jnp.exp(gc)).astype(BF16)
        k_inv = (k * jnp.exp(-gc)).astype(BF16)
        k_end = (k * jnp.exp(g_end - gc)).astype(BF16)
        decay = jnp.exp(g_end)
        v = v_ref[rows, :]
        for h in range(heads):
            sl = slice(h * kdim, (h + 1) * kdim)
            a = lax.dot_general(q_dec[:, sl], k_inv[:, sl], (((1,), (1,)), ((), ())),
                                preferred_element_type=F32)
            a = jnp.where(keep, a, 0.0).astype(BF16)
            st = st_ref[direction, h]
            o = (jnp.dot(a, v[:, sl], preferred_element_type=F32)
                 + lax.dot_general(q_dec[:, sl], st.astype(BF16), (((1,), (1,)), ((), ())),
                                   preferred_element_type=F32))
            o_ref[rows, sl] = o.astype(o_ref.dtype)
            st_ref[direction, h] = st * decay[:, sl] + lax.dot_general(
                v[:, sl], k_end[:, sl], (((0,), (0,)), ((), ())),
                preferred_element_type=F32)

    for s in range(nsub):
        chunk(0, qf_ref, zf_ref, vf_ref, of_ref, s * c)
        chunk(1, qb_ref, zb_ref, vb_ref, ob_ref, (nsub - 1 - s) * c)


def gla_bidirectional(proj, lb_logits, *, layer, batch, seq, heads, kdim, nsub=4):
    hk = heads * kdim
    tt = nsub * GLA_CHUNK
    nt = seq // tt
    nl = lb_logits.shape[0]

    def fwd(col):
        return pl.BlockSpec((None, tt, hk), lambda b, t: (b, t, col))

    def bwd(col):
        return pl.BlockSpec((None, tt, hk), lambda b, t: (b, nt - 1 - t, col))

    kern = functools.partial(_gla_kernel, layer=layer, heads=heads, kdim=kdim, nsub=nsub)
    return pl.pallas_call(
        kern,
        grid=(batch, nt),
        in_specs=[pl.BlockSpec((nl, hk), lambda b, t: (0, 0)),
                  fwd(0), fwd(1), fwd(3), bwd(0), bwd(2), bwd(3)],
        out_specs=[pl.BlockSpec((None, tt, hk), lambda b, t: (b, t, 0)),
                   pl.BlockSpec((None, tt, hk), lambda b, t: (b, nt - 1 - t, 0))],
        out_shape=[jax.ShapeDtypeStruct((batch, seq, hk), BF16)] * 2,
        scratch_shapes=[pltpu.VMEM((2, heads, kdim, kdim), F32)],
        compiler_params=_params(dimension_semantics=("arbitrary", "arbitrary")),
        name="gla_bidirectional",
    )(lb_logits, proj, proj, proj, proj, proj, proj)


def _hgrn_out_kernel(of_ref, ob_ref, gate_ref, x_ref, on_ref, w_ref, pg_ref, h_ref,
                     *, heads, vdim):
    o = of_ref[...].astype(F32) + ob_ref[...].astype(F32)
    parts = []
    for h in range(heads):
        oh = o[:, h * vdim:(h + 1) * vdim]
        ms = jnp.mean(oh * oh, axis=-1, keepdims=True)
        parts.append(oh * lax.rsqrt(ms + EPS))
    on = jnp.concatenate(parts, axis=-1) * on_ref[...]
    gate = gate_ref[...].astype(F32)
    y = (on * (gate * jax.nn.sigmoid(gate))).astype(BF16)
    m = jnp.dot(y, w_ref[...], preferred_element_type=F32)
    h_ref[...] = x_ref[...] + _rms(m, pg_ref[...])


def hgrn_out(o_f, o_b, proj, x, o_norm, w_out, post_gain, *, heads, vdim, tm=512):
    t, d = x.shape
    hv = heads * vdim
    row = lambda i: (i, 0)
    kern = functools.partial(_hgrn_out_kernel, heads=heads, vdim=vdim)
    return pl.pallas_call(
        kern,
        grid=(t // tm,),
        in_specs=[pl.BlockSpec((tm, hv), row), pl.BlockSpec((tm, hv), row),
                  pl.BlockSpec((tm, hv), lambda i: (i, 4)),
                  pl.BlockSpec((tm, d), row),
                  _const_spec((1, hv)), _const_spec((hv, d)), _const_spec((1, d))],
        out_specs=pl.BlockSpec((tm, d), row),
        out_shape=jax.ShapeDtypeStruct((t, d), F32),
        compiler_params=_params(dimension_semantics=("arbitrary",)),
        name="hgrn_out",
    )(o_f, o_b, proj, x, o_norm, w_out, post_gain)


def _mlp_ple_kernel(h_ref, p_ref, g1_ref, w1_ref, w2_ref, g2_ref, wp_ref, wg_ref, g3_ref,
                    o_ref, *, tf):
    h = h_ref[...]
    a = _rms(h, g1_ref[...]).astype(BF16)
    dff = w1_ref.shape[1]
    acc = jnp.zeros(h.shape, F32)
    for c in range(dff // tf):
        u = jnp.dot(a, w1_ref[:, c * tf:(c + 1) * tf], preferred_element_type=F32)
        u = jnp.maximum(u, 0.0)
        acc = acc + jnp.dot((u * u).astype(BF16), w2_ref[c * tf:(c + 1) * tf, :],
                            preferred_element_type=F32)
    h2 = h + _rms(acc, g2_ref[...])
    e = jnp.dot(p_ref[...].astype(BF16), wp_ref[...], preferred_element_type=F32)
    gt = jax.nn.sigmoid(jnp.dot(h2.astype(BF16), wg_ref[...], preferred_element_type=F32))
    o_ref[...] = h2 + _rms(gt * e, g3_ref[...])


def mlp_ple(h, p, g1, w1, w2, g2, wp, wg, g3, *, tm=512, tf=1024):
    t, d = h.shape
    dff = w1.shape[1]
    pd = p.shape[1]
    row = lambda i: (i, 0)
    return pl.pallas_call(
        functools.partial(_mlp_ple_kernel, tf=tf),
        grid=(t // tm,),
        in_specs=[pl.BlockSpec((tm, d), row), pl.BlockSpec((tm, pd), row),
                  _const_spec((1, d)), _const_spec((d, dff)), _const_spec((dff, d)),
                  _const_spec((1, d)), _const_spec((pd, d)), _const_spec((d, d)),
                  _const_spec((1, d))],
        out_specs=pl.BlockSpec((tm, d), row),
        out_shape=jax.ShapeDtypeStruct((t, d), F32),
        compiler_params=_params(dimension_semantics=("arbitrary",)),
        name="mlp_ple",
    )(h, p, g1, w1, w2, g2, wp, wg, g3)


def _mla_in_kernel(h_ref, pos_ref, inv_ref, g_ref, win_ref, qg_ref, kvg_ref,
                   wn_ref, wr_ref, wrr_ref, wuk_ref, wuv_ref,
                   q_ref, k_ref, v_ref, *, heads, q_rank, kv_rank, qk_scale):
    a = _rms(h_ref[...], g_ref[...]).astype(BF16)
    proj = jnp.dot(a, win_ref[...], preferred_element_type=F32)
    cq = _rms(proj[:, :q_rank], qg_ref[...]).astype(BF16)
    c0 = q_rank + kv_rank
    ckv = _rms(proj[:, q_rank:c0], kvg_ref[...]).astype(BF16)
    ang = pos_ref[...].astype(F32) * inv_ref[...]
    cos = jnp.cos(ang)
    sin = jnp.sin(ang)
    kr = (proj[:, c0:c0 + LANE] * cos + proj[:, c0 + LANE:c0 + 2 * LANE] * sin).astype(BF16)
    qn = jnp.dot(cq, wn_ref[...], preferred_element_type=F32) * qk_scale
    qr = jnp.dot(cq, wr_ref[...], preferred_element_type=F32)
    qrr = jnp.dot(cq, wrr_ref[...], preferred_element_type=F32)
    kn = jnp.dot(ckv, wuk_ref[...], preferred_element_type=F32)
    v_ref[...] = jnp.dot(ckv, wuv_ref[...], preferred_element_type=F32).astype(BF16)
    cos_s = cos * qk_scale
    sin_s = sin * qk_scale
    for h in range(heads):
        sl = slice(h * LANE, (h + 1) * LANE)
        q_ref[:, 2 * h * LANE:(2 * h + 1) * LANE] = qn[:, sl].astype(BF16)
        q_ref[:, (2 * h + 1) * LANE:(2 * h + 2) * LANE] = (
            qr[:, sl] * cos_s + qrr[:, sl] * sin_s).astype(BF16)
        k_ref[:, 2 * h * LANE:(2 * h + 1) * LANE] = kn[:, sl].astype(BF16)
        k_ref[:, (2 * h + 1) * LANE:(2 * h + 2) * LANE] = kr


def mla_in(h, pos, inv, gain, w_in, q_gain, kv_gain, wn, wr, wrr, wuk, wuv,
           *, heads, q_rank, kv_rank, qk_scale, tm=256):
    t, d = h.shape
    row = lambda i: (i, 0)
    kern = functools.partial(_mla_in_kernel, heads=heads, q_rank=q_rank, kv_rank=kv_rank,
                             qk_scale=qk_scale)
    consts = [inv, gain, w_in, q_gain, kv_gain, wn, wr, wrr, wuk, wuv]
    return pl.pallas_call(
        kern,
        grid=(t // tm,),
        in_specs=[pl.BlockSpec((tm, d), row), pl.BlockSpec((tm, 1), row)]
                 + [_const_spec(c.shape) for c in consts],
        out_specs=[pl.BlockSpec((tm, 2 * heads * LANE), row),
                   pl.BlockSpec((tm, 2 * heads * LANE), row),
                   pl.BlockSpec((tm, heads * LANE), row)],
        out_shape=[jax.ShapeDtypeStruct((t, 2 * heads * LANE), BF16),
                   jax.ShapeDtypeStruct((t, 2 * heads * LANE), BF16),
                   jax.ShapeDtypeStruct((t, heads * LANE), BF16)],
        compiler_params=_params(dimension_semantics=("arbitrary",)),
        name="mla_in",
    )(h, pos, *consts)


def _attn_kernel(q_ref, k_ref, v_ref, o_ref, *, tk):
    q = q_ref[...]
    tq = q.shape[0]
    nk = k_ref.shape[0] // tk

    def body(j, carry):
        m, l, acc = carry
        rows = pl.ds(pl.multiple_of(j * tk, tk), tk)
        s = lax.dot_general(q, k_ref[rows, :], (((1,), (1,)), ((), ())),
                            preferred_element_type=F32)
        m_new = jnp.maximum(m, jnp.max(s, axis=-1, keepdims=True))
        alpha = jnp.exp2(m - m_new)
        p = jnp.exp2(s - m_new)
        l = alpha * l + jnp.sum(p, axis=-1, keepdims=True)
        acc = alpha * acc + jnp.dot(p.astype(BF16), v_ref[rows, :],
                                    preferred_element_type=F32)
        return m_new, l, acc

    m0 = jnp.full((tq, 1), -jnp.inf, F32)
    l0 = jnp.zeros((tq, 1), F32)
    acc0 = jnp.zeros((tq, v_ref.shape[1]), F32)
    _, l, acc = lax.fori_loop(0, nk, body, (m0, l0, acc0))
    o_ref[...] = (acc / l).astype(o_ref.dtype)


def attention(q, k, v, *, batch, seq, heads, tq=512, tk=512):
    dq = q.shape[-1] // heads
    dv = v.shape[-1] // heads
    return pl.pallas_call(
        functools.partial(_attn_kernel, tk=tk),
        grid=(batch, heads, seq // tq),
        in_specs=[pl.BlockSpec((None, tq, dq), lambda b, h, i: (b, i, h)),
                  pl.BlockSpec((None, seq, dq), lambda b, h, i: (b, 0, h)),
                  pl.BlockSpec((None, seq, dv), lambda b, h, i: (b, 0, h))],
        out_specs=pl.BlockSpec((None, tq, dv), lambda b, h, i: (b, i, h)),
        out_shape=jax.ShapeDtypeStruct((batch, seq, heads * dv), BF16),
        compiler_params=_params(dimension_semantics=("arbitrary",) * 3),
        name="attention",
    )(q, k, v)


def _attn_out_kernel(o_ref, h_ref, w_ref, g_ref, out_ref):
    m = jnp.dot(o_ref[...], w_ref[...], preferred_element_type=F32)
    out_ref[...] = h_ref[...] + _rms(m, g_ref[...])


def attn_out(o, h, w_o, gain, *, tm=512):
    t, d = h.shape
    n = o.shape[1]
    row = lambda i: (i, 0)
    return pl.pallas_call(
        _attn_out_kernel,
        grid=(t // tm,),
        in_specs=[pl.BlockSpec((tm, n), row), pl.BlockSpec((tm, d), row),
                  _const_spec((n, d)), _const_spec((1, d))],
        out_specs=pl.BlockSpec((tm, d), row),
        out_shape=jax.ShapeDtypeStruct((t, d), F32),
        compiler_params=_params(dimension_semantics=("arbitrary",)),
        name="attn_out",
    )(o, h, w_o, gain)


def _rot_half_cols(w):
    half = w.shape[-1] // 2
    return jnp.concatenate([-w[..., half:], w[..., :half]], axis=-1)


def _pad_last(w, width):
    return jnp.pad(w, [(0, 0)] * (w.ndim - 1) + [(0, width - w.shape[-1])])


def kernel(x, p, positions, pre_mix_norm, post_mix_norm, pre_mlp_norm, post_mlp_norm,
           w_mlp_in, w_mlp_out, w_ple_proj, w_ple_gate, ple_norm,
           hg_lb_logits, hg_w_in, hg_o_norm, hg_w_out,
           mla_w_in, mla_q_norm, mla_w_uq, mla_kv_norm, mla_w_ukv, mla_w_o):
    batch, seq, d = x.shape
    depth = p.shape[0]
    t = batch * seq
    hg_heads, hg_vdim = hg_o_norm.shape[1], hg_o_norm.shape[2]
    hg_kdim = hg_lb_logits.shape[1] // hg_heads
    q_rank = mla_q_norm.shape[1]
    kv_rank = mla_kv_norm.shape[1]
    rope = mla_w_in.shape[2] - q_rank - kv_rank
    mla_heads = mla_w_o.shape[1] // LANE
    nope = mla_w_uq.shape[2] // mla_heads - rope
    vdim = mla_w_ukv.shape[2] // mla_heads - nope
    assert nope == LANE and vdim == LANE and 2 * rope == LANE

    row = lambda g: g.reshape(1, -1).astype(F32)
    h = x.reshape(t, d)
    pos = positions.reshape(t, 1)
    inv = 1.0 / (ROPE_BASE ** (jnp.arange(0, rope, 2, dtype=F32) / rope))
    inv = jnp.tile(inv, LANE // inv.shape[0]).reshape(1, LANE)
    qk_scale = (nope + rope) ** -0.5 * math.log2(math.e)

    for i in range(depth):
        j = i // 2
        if i % 2 == 0:
            proj = norm_matmul(h, row(pre_mix_norm[i]), hg_w_in[j].astype(BF16))
            o_f, o_b = gla_bidirectional(
                proj.reshape(batch, seq, -1), hg_lb_logits.astype(F32), layer=i,
                batch=batch, seq=seq, heads=hg_heads, kdim=hg_kdim)
            h = hgrn_out(o_f.reshape(t, -1), o_b.reshape(t, -1), proj, h,
                         row(hg_o_norm[j]), hg_w_out[j].astype(BF16), row(post_mix_norm[i]),
                         heads=hg_heads, vdim=hg_vdim)
        else:
            w_in = mla_w_in[j]
            c0 = q_rank + kv_rank
            w_kr = w_in[:, c0:]
            w_in_ext = jnp.concatenate(
                [w_in[:, :c0], _pad_last(w_kr, LANE), _pad_last(_rot_half_cols(w_kr), LANE)],
                axis=-1).astype(BF16)
            w_uq = mla_w_uq[j].reshape(q_rank, mla_heads, nope + rope)
            wn = w_uq[..., :nope].reshape(q_rank, -1).astype(BF16)
            w_qr = w_uq[..., nope:]
            wr = _pad_last(w_qr, LANE).reshape(q_rank, -1).astype(BF16)
            wrr = _pad_last(_rot_half_cols(w_qr), LANE).reshape(q_rank, -1).astype(BF16)
            w_ukv = mla_w_ukv[j].reshape(kv_rank, mla_heads, nope + vdim)
            wuk = w_ukv[..., :nope].reshape(kv_rank, -1).astype(BF16)
            wuv = w_ukv[..., nope:].reshape(kv_rank, -1).astype(BF16)
            q, k, v = mla_in(h, pos, inv, row(pre_mix_norm[i]), w_in_ext,
                             row(mla_q_norm[j]), row(mla_kv_norm[j]), wn, wr, wrr, wuk, wuv,
                             heads=mla_heads, q_rank=q_rank, kv_rank=kv_rank,
                             qk_scale=qk_scale)
            o = attention(q.reshape(batch, seq, -1), k.reshape(batch, seq, -1),
                          v.reshape(batch, seq, -1), batch=batch, seq=seq, heads=mla_heads)
            h = attn_out(o.reshape(t, -1), h, mla_w_o[j].astype(BF16), row(post_mix_norm[i]))
        h = mlp_ple(h, p[i].reshape(t, -1), row(pre_mlp_norm[i]),
                    w_mlp_in[i].astype(BF16), w_mlp_out[i].astype(BF16),
                    row(post_mlp_norm[i]), w_ple_proj[i].astype(BF16),
                    w_ple_gate[i].astype(BF16), row(ple_norm[i]))
    return h.reshape(batch, seq, d)
```

```python
import functools
import math

import jax
import jax.numpy as jnp
from jax import lax
from jax.experimental import pallas as pl
from jax.experimental.pallas import tpu as pltpu

EPS = 1e-6
ROPE_BASE = 10000.0
LANE = 128
BF16_SUBLANES = 16
GLA_CHUNK = 64
VMEM_LIMIT = 56 * 1024 * 1024

F32 = jnp.float32
BF16 = jnp.bfloat16


def _rms(x, gain):
    ms = jnp.mean(x * x, axis=-1, keepdims=True)
    return x * lax.rsqrt(ms + EPS) * gain


def _params(**kw):
    return pltpu.CompilerParams(vmem_limit_bytes=VMEM_LIMIT, **kw)


def _const_spec(shape):
    nd = len(shape)
    return pl.BlockSpec(shape, lambda *_: (0,) * nd, pipeline_mode=pl.Buffered(1))


def _norm_matmul_kernel(x_ref, g_ref, w_ref, o_ref, xn_ref):
    @pl.when(pl.program_id(1) == 0)
    def _():
        xn_ref[...] = _rms(x_ref[...], g_ref[...]).astype(BF16)

    o_ref[...] = jnp.dot(xn_ref[...], w_ref[...],
                         preferred_element_type=F32).astype(o_ref.dtype)


def norm_matmul(x, gain, w, *, tm=1024, tn=1024):
    t, d = x.shape
    n = w.shape[1]
    return pl.pallas_call(
        _norm_matmul_kernel,
        grid=(t // tm, n // tn),
        in_specs=[
            pl.BlockSpec((tm, d), lambda i, j: (i, 0)),
            pl.BlockSpec((1, d), lambda i, j: (0, 0)),
            pl.BlockSpec((d, tn), lambda i, j: (0, j)),
        ],
        out_specs=pl.BlockSpec((tm, tn), lambda i, j: (i, j)),
        out_shape=jax.ShapeDtypeStruct((t, n), BF16),
        scratch_shapes=[pltpu.VMEM((tm, d), BF16)],
        compiler_params=_params(dimension_semantics=("arbitrary", "arbitrary")),
        name="norm_matmul",
    )(x, gain, w)


def _gla_kernel(lbl_ref, qf_ref, zf_ref, vf_ref, qb_ref, zb_ref, vb_ref,
                of_ref, ob_ref, st_ref, *, layer, heads, kdim, nsub):
    c = GLA_CHUNK

    @pl.when(pl.program_id(1) == 0)
    def _():
        st_ref[...] = jnp.zeros_like(st_ref)

    lg = lbl_ref[...].astype(F32)
    e = jnp.exp(lg - jnp.max(lg, axis=0, keepdims=True))
    lb = (jnp.sum(e[:layer + 1], axis=0, keepdims=True)
          / jnp.sum(e, axis=0, keepdims=True))

    row = lax.broadcasted_iota(jnp.int32, (c, c), 0)
    col = lax.broadcasted_iota(jnp.int32, (c, c), 1)
    scale = kdim ** -0.5

    def chunk(direction, q_ref, z_ref, v_ref, o_ref, r0):
        keep = (col <= row) if direction == 0 else (col >= row)
        tri = keep.astype(BF16)
        rows = pl.ds(r0, c)
        z = z_ref[rows, :].astype(F32)
        f = lb + (1.0 - lb) * jax.nn.sigmoid(z)
        k = 1.0 - f
        g = jnp.log(f)
        g_hi = g.astype(BF16)
        g_lo = (g - g_hi.astype(F32)).astype(BF16)
        gc = (jnp.dot(tri, g_hi, preferred_element_type=F32)
              + jnp.dot(tri, g_lo, preferred_element_type=F32))
        g_end = gc[c - 1:c, :] if direction == 0 else gc[0:1, :]
        q_dec = (q_ref[rows, :].astype(F32) * scale * jnp.exp(gc)).astype(BF16)
        k_inv = (k * jnp.exp(-gc)).astype(BF16)
        k_end = (k * jnp.exp(g_end - gc)).astype(BF16)
        decay = jnp.exp(g_end)
        v = v_ref[rows, :]
        for h in range(heads):
            sl = slice(h * kdim, (h + 1) * kdim)
            a = lax.dot_general(q_dec[:, sl], k_inv[:, sl], (((1,), (1,)), ((), ())),
                                preferred_element_type=F32)
            a = jnp.where(keep, a, 0.0).astype(BF16)
            st = st_ref[direction, h]
            o = (jnp.dot(a, v[:, sl], preferred_element_type=F32)
                 + lax.dot_general(q_dec[:, sl], st.astype(BF16), (((1,), (1,)), ((), ())),
                                   preferred_element_type=F32))
            o_ref[rows, sl] = o.astype(o_ref.dtype)
            st_ref[direction, h] = st * decay[:, sl] + lax.dot_general(
                v[:, sl], k_end[:, sl], (((0,), (0,)), ((), ())),
                preferred_element_type=F32)

    for s in range(nsub):
        chunk(0, qf_ref, zf_ref, vf_ref, of_ref, s * c)
        chunk(1, qb_ref, zb_ref, vb_ref, ob_ref, (nsub - 1 - s) * c)


def gla_bidirectional(proj, lb_logits, *, layer, batch, seq, heads, kdim, nsub=4):
    hk = heads * kdim
    tt = nsub * GLA_CHUNK
    nt = seq // tt
    nl = lb_logits.shape[0]

    def fwd(col):
        return pl.BlockSpec((None, tt, hk), lambda b, t: (b, t, col))

    def bwd(col):
        return pl.BlockSpec((None, tt, hk), lambda b, t: (b, nt - 1 - t, col))

    kern = functools.partial(_gla_kernel, layer=layer, heads=heads, kdim=kdim, nsub=nsub)
    return pl.pallas_call(
        kern,
        grid=(batch, nt),
        in_specs=[pl.BlockSpec((nl, hk), lambda b, t: (0, 0)),
                  fwd(0), fwd(1), fwd(3), bwd(0), bwd(2), bwd(3)],
        out_specs=[pl.BlockSpec((None, tt, hk), lambda b, t: (b, t, 0)),
                   pl.BlockSpec((None, tt, hk), lambda b, t: (b, nt - 1 - t, 0))],
        out_shape=[jax.ShapeDtypeStruct((batch, seq, hk), BF16)] * 2,
        scratch_shapes=[pltpu.VMEM((2, heads, kdim, kdim), F32)],
        compiler_params=_params(dimension_semantics=("arbitrary", "arbitrary")),
        name="gla_bidirectional",
    )(lb_logits, proj, proj, proj, proj, proj, proj)


def _hgrn_out_kernel(of_ref, ob_ref, gate_ref, x_ref, on_ref, w_ref, pg_ref, h_ref,
                     *, heads, vdim):
    o = of_ref[...].astype(F32) + ob_ref[...].astype(F32)
    parts = []
    for h in range(heads):
        oh = o[:, h * vdim:(h + 1) * vdim]
        ms = jnp.mean(oh * oh, axis=-1, keepdims=True)
        parts.append(oh * lax.rsqrt(ms + EPS))
    on = jnp.concatenate(parts, axis=-1) * on_ref[...]
    gate = gate_ref[...].astype(F32)
    y = (on * (gate * jax.nn.sigmoid(gate))).astype(BF16)
    m = jnp.dot(y, w_ref[...], preferred_element_type=F32)
    h_ref[...] = x_ref[...] + _rms(m, pg_ref[...])


def hgrn_out(o_f, o_b, proj, x, o_norm, w_out, post_gain, *, heads, vdim, tm=512):
    t, d = x.shape
    hv = heads * vdim
    row = lambda i: (i, 0)
    kern = functools.partial(_hgrn_out_kernel, heads=heads, vdim=vdim)
    return pl.pallas_call(
        kern,
        grid=(t // tm,),
        in_specs=[pl.BlockSpec((tm, hv), row), pl.BlockSpec((tm, hv), row),
                  pl.BlockSpec((tm, hv), lambda i: (i, 4)),
                  pl.BlockSpec((tm, d), row),
                  _const_spec((1, hv)), _const_spec((hv, d)), _const_spec((1, d))],
        out_specs=pl.BlockSpec((tm, d), row),
        out_shape=jax.ShapeDtypeStruct((t, d), F32),
        compiler_params=_params(dimension_semantics=("arbitrary",)),
        name="hgrn_out",
    )(o_f, o_b, proj, x, o_norm, w_out, post_gain)


def _mlp_ple_kernel(h_ref, p_ref, g1_ref, w1_ref, w2_ref, g2_ref, wp_ref, wg_ref, g3_ref,
                    o_ref, *, tf):
    h = h_ref[...]
    a = _rms(h, g1_ref[...]).astype(BF16)
    dff = w1_ref.shape[1]
    acc = jnp.zeros(h.shape, F32)
    for c in range(dff // tf):
        u = jnp.dot(a, w1_ref[:, c * tf:(c + 1) * tf], preferred_element_type=F32)
        u = jnp.maximum(u, 0.0)
        acc = acc + jnp.dot((u * u).astype(BF16), w2_ref[c * tf:(c + 1) * tf, :],
                            preferred_element_type=F32)
    h2 = h + _rms(acc, g2_ref[...])
    e = jnp.dot(p_ref[...].astype(BF16), wp_ref[...], preferred_element_type=F32)
    gt = jax.nn.sigmoid(jnp.dot(h2.astype(BF16), wg_ref[...], preferred_element_type=F32))
    o_ref[...] = h2 + _rms(gt * e, g3_ref[...])


def mlp_ple(h, p, g1, w1, w2, g2, wp, wg, g3, *, tm=512, tf=1024):
    t, d = h.shape
    dff = w1.shape[1]
    pd = p.shape[1]
    row = lambda i: (i, 0)
    return pl.pallas_call(
        functools.partial(_mlp_ple_kernel, tf=tf),
        grid=(t // tm,),
        in_specs=[pl.BlockSpec((tm, d), row), pl.BlockSpec((tm, pd), row),
                  _const_spec((1, d)), _const_spec((d, dff)), _const_spec((dff, d)),
                  _const_spec((1, d)), _const_spec((pd, d)), _const_spec((d, d)),
                  _const_spec((1, d))],
        out_specs=pl.BlockSpec((tm, d), row),
        out_shape=jax.ShapeDtypeStruct((t, d), F32),
        compiler_params=_params(dimension_semantics=("arbitrary",)),
        name="mlp_ple",
    )(h, p, g1, w1, w2, g2, wp, wg, g3)


def _mla_in_kernel(h_ref, pos_ref, inv_ref, g_ref, win_ref, qg_ref, kvg_ref,
                   wn_ref, wr_ref, wrr_ref, wuk_ref, wuv_ref,
                   q_ref, k_ref, v_ref, *, heads, q_rank, kv_rank, qk_scale):
    a = _rms(h_ref[...], g_ref[...]).astype(BF16)
    proj = jnp.dot(a, win_ref[...], preferred_element_type=F32)
    cq = _rms(proj[:, :q_rank], qg_ref[...]).astype(BF16)
    c0 = q_rank + kv_rank
    ckv = _rms(proj[:, q_rank:c0], kvg_ref[...]).astype(BF16)
    ang = pos_ref[...].astype(F32) * inv_ref[...]
    cos = jnp.cos(ang)
    sin = jnp.sin(ang)
    kr = (proj[:, c0:c0 + LANE] * cos + proj[:, c0 + LANE:c0 + 2 * LANE] * sin).astype(BF16)
    qn = jnp.dot(cq, wn_ref[...], preferred_element_type=F32) * qk_scale
    qr = jnp.dot(cq, wr_ref[...], preferred_element_type=F32)
    qrr = jnp.dot(cq, wrr_ref[...], preferred_element_type=F32)
    kn = jnp.dot(ckv, wuk_ref[...], preferred_element_type=F32)
    v_ref[...] = lax.dot_general(wuv_ref[...], ckv, (((1,), (1,)), ((), ())),
                                 preferred_element_type=F32).astype(BF16)
    cos_s = cos * qk_scale
    sin_s = sin * qk_scale
    for h in range(heads):
        sl = slice(h * LANE, (h + 1) * LANE)
        q_ref[:, 2 * h * LANE:(2 * h + 1) * LANE] = qn[:, sl].astype(BF16)
        q_ref[:, (2 * h + 1) * LANE:(2 * h + 2) * LANE] = (
            qr[:, sl] * cos_s + qrr[:, sl] * sin_s).astype(BF16)
        k_ref[:, 2 * h * LANE:(2 * h + 1) * LANE] = kn[:, sl].astype(BF16)
        k_ref[:, (2 * h + 1) * LANE:(2 * h + 2) * LANE] = kr


def mla_in(h, pos, inv, gain, w_in, q_gain, kv_gain, wn, wr, wrr, wuk, wuv_t,
           *, batch, heads, q_rank, kv_rank, qk_scale, tm=256):
    t, d = h.shape
    nsb = t // batch // tm
    row = lambda i: (i, 0)
    kern = functools.partial(_mla_in_kernel, heads=heads, q_rank=q_rank, kv_rank=kv_rank,
                             qk_scale=qk_scale)
    consts = [inv, gain, w_in, q_gain, kv_gain, wn, wr, wrr, wuk, wuv_t]
    return pl.pallas_call(
        kern,
        grid=(t // tm,),
        in_specs=[pl.BlockSpec((tm, d), row), pl.BlockSpec((tm, 1), row)]
                 + [_const_spec(c.shape) for c in consts],
        out_specs=[pl.BlockSpec((tm, 2 * heads * LANE), row),
                   pl.BlockSpec((tm, 2 * heads * LANE), row),
                   pl.BlockSpec((None, heads * LANE, tm), lambda i: (i // nsb, 0, i % nsb))],
        out_shape=[jax.ShapeDtypeStruct((t, 2 * heads * LANE), BF16),
                   jax.ShapeDtypeStruct((t, 2 * heads * LANE), BF16),
                   jax.ShapeDtypeStruct((batch, heads * LANE, t // batch), BF16)],
        compiler_params=_params(dimension_semantics=("arbitrary",)),
        name="mla_in",
    )(h, pos, *consts)


def _attn_kernel(q_ref, k_ref, vt_ref, o_ref, *, tq, tk, ahead):
    nq = q_ref.shape[0] // tq
    nk = k_ref.shape[0] // tk

    def scores(t):
        i, j = divmod(t, nk)
        return lax.dot_general(k_ref[j * tk:(j + 1) * tk, :], q_ref[i * tq:(i + 1) * tq, :],
                               (((1,), (1,)), ((), ())),
                               preferred_element_type=F32)

    total = nq * nk
    pending = [scores(t) for t in range(min(ahead, total))]
    for t in range(total):
        i, j = divmod(t, nk)
        s = pending.pop(0)
        if t + ahead < total:
            pending.append(scores(t + ahead))
        s_max = jnp.max(s, axis=0, keepdims=True)
        m_new = s_max if j == 0 else jnp.maximum(m, s_max)
        p = jnp.exp2(s - m_new)
        p_sum = jnp.sum(p, axis=0, keepdims=True)
        pv = jnp.dot(vt_ref[:, j * tk:(j + 1) * tk], p.astype(BF16),
                     preferred_element_type=F32)
        if j == 0:
            l, acc = p_sum, pv
        else:
            alpha = jnp.exp2(m - m_new)
            l = alpha * l + p_sum
            acc = alpha * acc + pv
        m = m_new
        if j == nk - 1:
            o_ref[i * tq:(i + 1) * tq, :] = (acc / l).T.astype(o_ref.dtype)


def attention(q, k, vt, *, batch, seq, heads, tq=512, nq=4, tk=1024, ahead=2):
    dq = q.shape[-1] // heads
    dv = vt.shape[1] // heads
    tqb = tq * nq
    return pl.pallas_call(
        functools.partial(_attn_kernel, tq=tq, tk=tk, ahead=ahead),
        grid=(batch, heads, seq // tqb),
        in_specs=[pl.BlockSpec((None, tqb, dq), lambda b, h, i: (b, i, h)),
                  pl.BlockSpec((None, seq, dq), lambda b, h, i: (b, 0, h)),
                  pl.BlockSpec((None, dv, seq), lambda b, h, i: (b, h, 0))],
        out_specs=pl.BlockSpec((None, tqb, dv), lambda b, h, i: (b, i, h)),
        out_shape=jax.ShapeDtypeStruct((batch, seq, heads * dv), BF16),
        compiler_params=_params(dimension_semantics=("arbitrary",) * 3),
        name="attention",
    )(q, k, vt)


def _attn_out_kernel(o_ref, h_ref, w_ref, g_ref, out_ref):
    m = jnp.dot(o_ref[...], w_ref[...], preferred_element_type=F32)
    out_ref[...] = h_ref[...] + _rms(m, g_ref[...])


def attn_out(o, h, w_o, gain, *, tm=512):
    t, d = h.shape
    n = o.shape[1]
    row = lambda i: (i, 0)
    return pl.pallas_call(
        _attn_out_kernel,
        grid=(t // tm,),
        in_specs=[pl.BlockSpec((tm, n), row), pl.BlockSpec((tm, d), row),
                  _const_spec((n, d)), _const_spec((1, d))],
        out_specs=pl.BlockSpec((tm, d), row),
        out_shape=jax.ShapeDtypeStruct((t, d), F32),
        compiler_params=_params(dimension_semantics=("arbitrary",)),
        name="attn_out",
    )(o, h, w_o, gain)


def _rot_half_cols(w):
    half = w.shape[-1] // 2
    return jnp.concatenate([-w[..., half:], w[..., :half]], axis=-1)


def _pad_last(w, width):
    return jnp.pad(w, [(0, 0)] * (w.ndim - 1) + [(0, width - w.shape[-1])])


def kernel(x, p, positions, pre_mix_norm, post_mix_norm, pre_mlp_norm, post_mlp_norm,
           w_mlp_in, w_mlp_out, w_ple_proj, w_ple_gate, ple_norm,
           hg_lb_logits, hg_w_in, hg_o_norm, hg_w_out,
           mla_w_in, mla_q_norm, mla_w_uq, mla_kv_norm, mla_w_ukv, mla_w_o):
    batch, seq, d = x.shape
    depth = p.shape[0]
    t = batch * seq
    hg_heads, hg_vdim = hg_o_norm.shape[1], hg_o_norm.shape[2]
    hg_kdim = hg_lb_logits.shape[1] // hg_heads
    q_rank = mla_q_norm.shape[1]
    kv_rank = mla_kv_norm.shape[1]
    rope = mla_w_in.shape[2] - q_rank - kv_rank
    mla_heads = mla_w_o.shape[1] // LANE
    nope = mla_w_uq.shape[2] // mla_heads - rope
    vdim = mla_w_ukv.shape[2] // mla_heads - nope
    assert nope == LANE and vdim == LANE and 2 * rope == LANE

    row = lambda g: g.reshape(1, -1).astype(F32)
    h = x.reshape(t, d)
    pos = positions.reshape(t, 1)
    inv = 1.0 / (ROPE_BASE ** (jnp.arange(0, rope, 2, dtype=F32) / rope))
    inv = jnp.tile(inv, LANE // inv.shape[0]).reshape(1, LANE)
    qk_scale = (nope + rope) ** -0.5 * math.log2(math.e)

    for i in range(depth):
        j = i // 2
        if i % 2 == 0:
            proj = norm_matmul(h, row(pre_mix_norm[i]), hg_w_in[j].astype(BF16))
            o_f, o_b = gla_bidirectional(
                proj.reshape(batch, seq, -1), hg_lb_logits.astype(F32), layer=i,
                batch=batch, seq=seq, heads=hg_heads, kdim=hg_kdim)
            h = hgrn_out(o_f.reshape(t, -1), o_b.reshape(t, -1), proj, h,
                         row(hg_o_norm[j]), hg_w_out[j].astype(BF16), row(post_mix_norm[i]),
                         heads=hg_heads, vdim=hg_vdim)
        else:
            w_in = mla_w_in[j]
            c0 = q_rank + kv_rank
            w_kr = w_in[:, c0:]
            w_in_ext = jnp.concatenate(
                [w_in[:, :c0], _pad_last(w_kr, LANE), _pad_last(_rot_half_cols(w_kr), LANE)],
                axis=-1).astype(BF16)
            w_uq = mla_w_uq[j].reshape(q_rank, mla_heads, nope + rope)
            wn = w_uq[..., :nope].reshape(q_rank, -1).astype(BF16)
            w_qr = w_uq[..., nope:]
            wr = _pad_last(w_qr, LANE).reshape(q_rank, -1).astype(BF16)
            wrr = _pad_last(_rot_half_cols(w_qr), LANE).reshape(q_rank, -1).astype(BF16)
            w_ukv = mla_w_ukv[j].reshape(kv_rank, mla_heads, nope + vdim)
            wuk = w_ukv[..., :nope].reshape(kv_rank, -1).astype(BF16)
            wuv_t = w_ukv[..., nope:].reshape(kv_rank, -1).T.astype(BF16)
            q, k, vt = mla_in(h, pos, inv, row(pre_mix_norm[i]), w_in_ext,
                              row(mla_q_norm[j]), row(mla_kv_norm[j]), wn, wr, wrr, wuk, wuv_t,
                              batch=batch, heads=mla_heads, q_rank=q_rank, kv_rank=kv_rank,
                              qk_scale=qk_scale)
            o = attention(q.reshape(batch, seq, -1), k.reshape(batch, seq, -1), vt,
                          batch=batch, seq=seq, heads=mla_heads)
            h = attn_out(o.reshape(t, -1), h, mla_w_o[j].astype(BF16), row(post_mix_norm[i]))
        h = mlp_ple(h, p[i].reshape(t, -1), row(pre_mlp_norm[i]),
                    w_mlp_in[i].astype(BF16), w_mlp_out[i].astype(BF16),
                    row(post_mlp_norm[i]), w_ple_proj[i].astype(BF16),
                    w_ple_gate[i].astype(BF16), row(ple_norm[i]))
    return h.reshape(batch, seq, d)
```

```python
import functools
import math

import jax
import jax.numpy as jnp
from jax import lax
from jax.experimental import pallas as pl
from jax.experimental.pallas import tpu as pltpu

EPS = 1e-6
ROPE_BASE = 10000.0
LANE = 128
GLA_CHUNK = 64
VMEM_LIMIT = 56 * 1024 * 1024
TAIL_ROWS = 512

F32 = jnp.float32
BF16 = jnp.bfloat16


def _rms(x, gain):
    ms = jnp.mean(x * x, axis=-1, keepdims=True)
    return x * lax.rsqrt(ms + EPS) * gain


def _params(**kw):
    return pltpu.CompilerParams(vmem_limit_bytes=VMEM_LIMIT, **kw)


def _const_spec(shape):
    nd = len(shape)
    return pl.BlockSpec(shape, lambda *_: (0,) * nd, pipeline_mode=pl.Buffered(1))


def _norm_matmul_kernel(x_ref, g_ref, w_ref, o_ref, *, tn):
    xn = _rms(x_ref[...], g_ref[...]).astype(BF16)
    for c in range(w_ref.shape[1] // tn):
        cols = slice(c * tn, (c + 1) * tn)
        o_ref[:, cols] = jnp.dot(xn, w_ref[:, cols],
                                 preferred_element_type=F32).astype(o_ref.dtype)


def norm_matmul(x, gain, w, *, tm=512, tn=1024):
    t, d = x.shape
    n = w.shape[1]
    return pl.pallas_call(
        functools.partial(_norm_matmul_kernel, tn=tn),
        grid=(t // tm,),
        in_specs=[pl.BlockSpec((tm, d), lambda i: (i, 0)),
                  _const_spec((1, d)), _const_spec((d, n))],
        out_specs=pl.BlockSpec((tm, n), lambda i: (i, 0)),
        out_shape=jax.ShapeDtypeStruct((t, n), BF16),
        compiler_params=_params(dimension_semantics=("arbitrary",)),
        name="norm_matmul",
    )(x, gain, w)


def _gla_kernel(lbl_ref, qf_ref, zf_ref, vf_ref, qb_ref, zb_ref, vb_ref,
                of_ref, ob_ref, st_ref, *, layer, heads, kdim, nsub):
    c = GLA_CHUNK

    @pl.when(pl.program_id(1) == 0)
    def _():
        st_ref[...] = jnp.zeros_like(st_ref)

    lg = lbl_ref[...].astype(F32)
    e = jnp.exp(lg - jnp.max(lg, axis=0, keepdims=True))
    lb = (jnp.sum(e[:layer + 1], axis=0, keepdims=True)
          / jnp.sum(e, axis=0, keepdims=True))

    row = lax.broadcasted_iota(jnp.int32, (c, c), 0)
    col = lax.broadcasted_iota(jnp.int32, (c, c), 1)
    keeps = (col <= row, col >= row)
    scale = kdim ** -0.5
    nt_dims = (((1,), (1,)), ((), ()))
    refs = ((qf_ref, zf_ref, vf_ref, of_ref), (qb_ref, zb_ref, vb_ref, ob_ref))

    def rows_of(direction, s):
        return pl.ds((s if direction == 0 else nsub - 1 - s) * c, c)

    def prep(direction, s):
        q_ref, z_ref, _, _ = refs[direction]
        rows = rows_of(direction, s)
        f = lb + (1.0 - lb) * jax.nn.sigmoid(z_ref[rows, :].astype(F32))
        g = jnp.log2(f)
        g_hi = g.astype(BF16)
        g_lo = (g - g_hi.astype(F32)).astype(BF16)
        tri = keeps[direction].astype(BF16)
        gc = (jnp.dot(tri, g_hi, preferred_element_type=F32)
              + jnp.dot(tri, g_lo, preferred_element_type=F32))
        decay = jnp.exp2(gc[c - 1:c, :] if direction == 0 else gc[0:1, :])
        q_dec = (q_ref[rows, :].astype(F32) * scale * jnp.exp2(gc)).astype(BF16)
        k_inv = (1.0 - f) * jnp.exp2(-gc)
        k_end = (k_inv * decay).astype(BF16)
        return q_dec, k_inv.astype(BF16), k_end, decay

    def state_stage(direction, s, prepped):
        q_dec, k_inv, k_end, decay = prepped
        v = refs[direction][2][rows_of(direction, s), :]
        out = []
        for h in range(heads):
            sl = slice(h * kdim, (h + 1) * kdim)
            a = lax.dot_general(q_dec[:, sl], k_inv[:, sl], nt_dims,
                                preferred_element_type=F32)
            st = st_ref[direction, h]
            o_state = lax.dot_general(q_dec[:, sl], st.astype(BF16), nt_dims,
                                      preferred_element_type=F32)
            st_ref[direction, h] = st * decay[:, sl] + lax.dot_general(
                v[:, sl], k_end[:, sl], (((0,), (0,)), ((), ())),
                preferred_element_type=F32)
            out.append((a, o_state))
        return out

    def output_stage(direction, s, staged):
        _, _, v_ref, o_ref = refs[direction]
        rows = rows_of(direction, s)
        v = v_ref[rows, :]
        for h in range(heads):
            sl = slice(h * kdim, (h + 1) * kdim)
            a, o_state = staged[h]
            a = jnp.where(keeps[direction], a, 0.0).astype(BF16)
            o = jnp.dot(a, v[:, sl], preferred_element_type=F32) + o_state
            o_ref[rows, sl] = o.astype(o_ref.dtype)

    dirs = (0, 1)
    prepped = [prep(d, 0) for d in dirs]
    staged_prev = None
    for s in range(nsub):
        prepped_next = [prep(d, s + 1) for d in dirs] if s + 1 < nsub else None
        staged = [state_stage(d, s, prepped[d]) for d in dirs]
        if staged_prev is not None:
            for d in dirs:
                output_stage(d, s - 1, staged_prev[d])
        staged_prev, prepped = staged, prepped_next
    for d in dirs:
        output_stage(d, nsub - 1, staged_prev[d])


def gla_bidirectional(proj, lb_logits, *, layer, batch, seq, heads, kdim, nsub=8):
    hk = heads * kdim
    tt = nsub * GLA_CHUNK
    nt = seq // tt
    nl = lb_logits.shape[0]

    def fwd(col):
        return pl.BlockSpec((None, tt, hk), lambda b, t: (b, t, col))

    def bwd(col):
        return pl.BlockSpec((None, tt, hk), lambda b, t: (b, nt - 1 - t, col))

    kern = functools.partial(_gla_kernel, layer=layer, heads=heads, kdim=kdim, nsub=nsub)
    return pl.pallas_call(
        kern,
        grid=(batch, nt),
        in_specs=[pl.BlockSpec((nl, hk), lambda b, t: (0, 0)),
                  fwd(0), fwd(1), fwd(3), bwd(0), bwd(2), bwd(3)],
        out_specs=[pl.BlockSpec((None, tt, hk), lambda b, t: (b, t, 0)),
                   pl.BlockSpec((None, tt, hk), lambda b, t: (b, nt - 1 - t, 0))],
        out_shape=[jax.ShapeDtypeStruct((batch, seq, hk), BF16)] * 2,
        scratch_shapes=[pltpu.VMEM((2, heads, kdim, kdim), F32)],
        compiler_params=_params(dimension_semantics=("arbitrary", "arbitrary")),
        name="gla_bidirectional",
    )(lb_logits, proj, proj, proj, proj, proj, proj)


def _hgrn_gated(of_ref, ob_ref, gate_ref, on_ref, *, heads, vdim):
    o = of_ref[...].astype(F32) + ob_ref[...].astype(F32)
    parts = []
    for h in range(heads):
        oh = o[:, h * vdim:(h + 1) * vdim]
        ms = jnp.mean(oh * oh, axis=-1, keepdims=True)
        parts.append(oh * lax.rsqrt(ms + EPS))
    on = jnp.concatenate(parts, axis=-1) * on_ref[...]
    gate = gate_ref[...].astype(F32)
    return (on * (gate * jax.nn.sigmoid(gate))).astype(BF16)


def _layer_tail_kernel(*refs, hgrn_heads, hgrn_vdim, tf):
    if hgrn_heads:
        of_ref, ob_ref, gate_ref, on_ref = refs[:4]
        y = _hgrn_gated(of_ref, ob_ref, gate_ref, on_ref, heads=hgrn_heads, vdim=hgrn_vdim)
        refs = refs[4:]
    else:
        y = refs[0][...]
        refs = refs[1:]
    (h_ref, p_ref, wm_ref, gm_ref, g1_ref, w1_ref, w2_ref, g2_ref, wp_ref, wg_ref, g3_ref,
     o_ref) = refs
    h = h_ref[...] + _rms(jnp.dot(y, wm_ref[...], preferred_element_type=F32), gm_ref[...])
    a = _rms(h, g1_ref[...]).astype(BF16)
    dff = w1_ref.shape[1]
    acc = jnp.zeros(h.shape, F32)
    for c in range(dff // tf):
        u = jnp.dot(a, w1_ref[:, c * tf:(c + 1) * tf], preferred_element_type=F32)
        u = jnp.maximum(u, 0.0)
        acc = acc + jnp.dot((u * u).astype(BF16), w2_ref[c * tf:(c + 1) * tf, :],
                            preferred_element_type=F32)
    h2 = h + _rms(acc, g2_ref[...])
    e = jnp.dot(p_ref[...].astype(BF16), wp_ref[...], preferred_element_type=F32)
    gt = jax.nn.sigmoid(jnp.dot(h2.astype(BF16), wg_ref[...], preferred_element_type=F32))
    o_ref[...] = h2 + _rms(gt * e, g3_ref[...])


def layer_tail(mixer_in, mixer_specs, h, p, consts, *, hgrn_heads=0, hgrn_vdim=0,
               tm=512, tf=1024):
    t, d = h.shape
    pd = p.shape[1]
    row = lambda i: (i, 0)
    kern = functools.partial(_layer_tail_kernel, hgrn_heads=hgrn_heads, hgrn_vdim=hgrn_vdim,
                             tf=tf)
    return pl.pallas_call(
        kern,
        grid=(t // tm,),
        in_specs=list(mixer_specs)
                 + [pl.BlockSpec((tm, d), row), pl.BlockSpec((tm, pd), row)]
                 + [_const_spec(c.shape) for c in consts],
        out_specs=pl.BlockSpec((tm, d), row),
        out_shape=jax.ShapeDtypeStruct((t, d), F32),
        compiler_params=_params(dimension_semantics=("arbitrary",)),
        name="layer_tail",
    )(*mixer_in, h, p, *consts)


def _mla_in_kernel(h_ref, pos_ref, inv_ref, g_ref, win_ref, qg_ref, kvg_ref,
                   wn_ref, wr_ref, wrr_ref, wuk_ref, wuv_ref,
                   q_ref, k_ref, v_ref, *, heads, q_rank, kv_rank, qk_scale):
    a = _rms(h_ref[...], g_ref[...]).astype(BF16)
    proj = jnp.dot(a, win_ref[...], preferred_element_type=F32)
    cq = _rms(proj[:, :q_rank], qg_ref[...]).astype(BF16)
    c0 = q_rank + kv_rank
    ckv = _rms(proj[:, q_rank:c0], kvg_ref[...]).astype(BF16)
    ang = pos_ref[...].astype(F32) * inv_ref[...]
    cos = jnp.cos(ang)
    sin = jnp.sin(ang)
    kr = proj[:, c0:c0 + LANE] * cos + proj[:, c0 + LANE:c0 + 2 * LANE] * sin
    low_half = lax.broadcasted_iota(jnp.int32, (1, LANE), 1) < LANE // 2
    kr_half = (jnp.where(low_half, kr, 0.0).astype(BF16),
               jnp.where(low_half, 0.0, kr).astype(BF16))
    qn = jnp.dot(cq, wn_ref[...], preferred_element_type=F32) * qk_scale
    cos_t = jnp.tile(cos * qk_scale, (1, heads // 2))
    sin_t = jnp.tile(sin * qk_scale, (1, heads // 2))
    qr = (jnp.dot(cq, wr_ref[...], preferred_element_type=F32) * cos_t
          + jnp.dot(cq, wrr_ref[...], preferred_element_type=F32) * sin_t)
    kn = jnp.dot(ckv, wuk_ref[...], preferred_element_type=F32)
    v_ref[...] = lax.dot_general(wuv_ref[...], ckv, (((1,), (1,)), ((), ())),
                                 preferred_element_type=F32).astype(BF16)
    for h in range(heads):
        sl = slice(h * LANE, (h + 1) * LANE)
        pair = qr[:, (h // 2) * LANE:(h // 2 + 1) * LANE]
        q_ref[:, 2 * h * LANE:(2 * h + 1) * LANE] = qn[:, sl].astype(BF16)
        q_ref[:, (2 * h + 1) * LANE:(2 * h + 2) * LANE] = jnp.where(
            low_half == (h % 2 == 0), pair, 0.0).astype(BF16)
        k_ref[:, 2 * h * LANE:(2 * h + 1) * LANE] = kn[:, sl].astype(BF16)
        k_ref[:, (2 * h + 1) * LANE:(2 * h + 2) * LANE] = kr_half[h % 2]


def mla_in(h, pos, inv, gain, w_in, q_gain, kv_gain, wn, wr, wrr, wuk, wuv_t,
           *, batch, heads, q_rank, kv_rank, qk_scale, tm=256):
    t, d = h.shape
    nsb = t // batch // tm
    row = lambda i: (i, 0)
    kern = functools.partial(_mla_in_kernel, heads=heads, q_rank=q_rank, kv_rank=kv_rank,
                             qk_scale=qk_scale)
    consts = [inv, gain, w_in, q_gain, kv_gain, wn, wr, wrr, wuk, wuv_t]
    return pl.pallas_call(
        kern,
        grid=(t // tm,),
        in_specs=[pl.BlockSpec((tm, d), row), pl.BlockSpec((tm, 1), row)]
                 + [_const_spec(c.shape) for c in consts],
        out_specs=[pl.BlockSpec((tm, 2 * heads * LANE), row),
                   pl.BlockSpec((tm, 2 * heads * LANE), row),
                   pl.BlockSpec((None, heads * LANE, tm), lambda i: (i // nsb, 0, i % nsb))],
        out_shape=[jax.ShapeDtypeStruct((t, 2 * heads * LANE), BF16),
                   jax.ShapeDtypeStruct((t, 2 * heads * LANE), BF16),
                   jax.ShapeDtypeStruct((batch, heads * LANE, t // batch), BF16)],
        compiler_params=_params(dimension_semantics=("arbitrary",)),
        name="mla_in",
    )(h, pos, *consts)


def _attn_kernel(q_ref, k_ref, vt_ref, o_ref, *, tq, tk, ahead):
    nq = q_ref.shape[0] // tq
    nk = k_ref.shape[0] // tk

    def scores(t):
        i, j = divmod(t, nk)
        return lax.dot_general(k_ref[j * tk:(j + 1) * tk, :], q_ref[i * tq:(i + 1) * tq, :],
                               (((1,), (1,)), ((), ())),
                               preferred_element_type=F32)

    total = nq * nk
    pending = [scores(t) for t in range(min(ahead, total))]
    for t in range(total):
        i, j = divmod(t, nk)
        s = pending.pop(0)
        if t + ahead < total:
            pending.append(scores(t + ahead))
        s_max = jnp.max(s, axis=0, keepdims=True)
        m_new = s_max if j == 0 else jnp.maximum(m, s_max)
        p = jnp.exp2(s - m_new)
        p_sum = jnp.sum(p, axis=0, keepdims=True)
        pv = jnp.dot(vt_ref[:, j * tk:(j + 1) * tk], p.astype(BF16),
                     preferred_element_type=F32)
        if j == 0:
            l, acc = p_sum, pv
        else:
            alpha = jnp.exp2(m - m_new)
            l = alpha * l + p_sum
            acc = alpha * acc + pv
        m = m_new
        if j == nk - 1:
            o_ref[i * tq:(i + 1) * tq, :] = (acc / l).T.astype(o_ref.dtype)


def attention(q, k, vt, *, batch, seq, heads, tq=512, nq=4, tk=1024, ahead=2):
    dq = q.shape[-1] // heads
    dv = vt.shape[1] // heads
    tqb = tq * nq
    return pl.pallas_call(
        functools.partial(_attn_kernel, tq=tq, tk=tk, ahead=ahead),
        grid=(batch, heads, seq // tqb),
        in_specs=[pl.BlockSpec((None, tqb, dq), lambda b, h, i: (b, i, h)),
                  pl.BlockSpec((None, seq, dq), lambda b, h, i: (b, 0, h)),
                  pl.BlockSpec((None, dv, seq), lambda b, h, i: (b, h, 0))],
        out_specs=pl.BlockSpec((None, tqb, dv), lambda b, h, i: (b, i, h)),
        out_shape=jax.ShapeDtypeStruct((batch, seq, heads * dv), BF16),
        compiler_params=_params(dimension_semantics=("arbitrary",) * 3),
        name="attention",
    )(q, k, vt)


def _rot_half_cols(w):
    half = w.shape[-1] // 2
    return jnp.concatenate([-w[..., half:], w[..., :half]], axis=-1)


def kernel(x, p, positions, pre_mix_norm, post_mix_norm, pre_mlp_norm, post_mlp_norm,
           w_mlp_in, w_mlp_out, w_ple_proj, w_ple_gate, ple_norm,
           hg_lb_logits, hg_w_in, hg_o_norm, hg_w_out,
           mla_w_in, mla_q_norm, mla_w_uq, mla_kv_norm, mla_w_ukv, mla_w_o):
    batch, seq, d = x.shape
    depth = p.shape[0]
    t = batch * seq
    hg_heads, hg_vdim = hg_o_norm.shape[1], hg_o_norm.shape[2]
    hg_kdim = hg_lb_logits.shape[1] // hg_heads
    q_rank = mla_q_norm.shape[1]
    kv_rank = mla_kv_norm.shape[1]
    rope = mla_w_in.shape[2] - q_rank - kv_rank
    mla_heads = mla_w_o.shape[1] // LANE
    nope = mla_w_uq.shape[2] // mla_heads - rope
    vdim = mla_w_ukv.shape[2] // mla_heads - nope
    assert nope == LANE and vdim == LANE and 2 * rope == LANE

    row = lambda g: g.reshape(1, -1).astype(F32)
    h = x.reshape(t, d)
    pos = positions.reshape(t, 1)
    inv = 1.0 / (ROPE_BASE ** (jnp.arange(0, rope, 2, dtype=F32) / rope))
    inv = jnp.tile(inv, LANE // inv.shape[0]).reshape(1, LANE)
    qk_scale = (nope + rope) ** -0.5 * math.log2(math.e)

    for i in range(depth):
        j = i // 2
        if i % 2 == 0:
            proj = norm_matmul(h, row(pre_mix_norm[i]), hg_w_in[j].astype(BF16))
            o_f, o_b = gla_bidirectional(
                proj.reshape(batch, seq, -1), hg_lb_logits.astype(F32), layer=i,
                batch=batch, seq=seq, heads=hg_heads, kdim=hg_kdim)
            hv = hg_heads * hg_vdim
            tile = lambda col: pl.BlockSpec((TAIL_ROWS, hv), lambda r: (r, col))
            mixer_in = (o_f.reshape(t, -1), o_b.reshape(t, -1), proj, row(hg_o_norm[j]))
            mixer_specs = (tile(0), tile(0), tile(4), _const_spec((1, hv)))
            w_mix, tail_kw = hg_w_out[j], dict(hgrn_heads=hg_heads, hgrn_vdim=hg_vdim)
        else:
            w_in = mla_w_in[j]
            c0 = q_rank + kv_rank
            w_kr = w_in[:, c0:]
            w_krr = _rot_half_cols(w_kr)
            w_in_ext = jnp.concatenate([w_in[:, :c0], w_kr, w_kr, w_krr, w_krr],
                                       axis=-1).astype(BF16)
            w_uq = mla_w_uq[j].reshape(q_rank, mla_heads, nope + rope)
            wn = w_uq[..., :nope].reshape(q_rank, -1).astype(BF16)
            w_qr = w_uq[..., nope:]
            wr = w_qr.reshape(q_rank, -1).astype(BF16)
            wrr = _rot_half_cols(w_qr).reshape(q_rank, -1).astype(BF16)
            w_ukv = mla_w_ukv[j].reshape(kv_rank, mla_heads, nope + vdim)
            wuk = w_ukv[..., :nope].reshape(kv_rank, -1).astype(BF16)
            wuv_t = w_ukv[..., nope:].reshape(kv_rank, -1).T.astype(BF16)
            q, k, vt = mla_in(h, pos, inv, row(pre_mix_norm[i]), w_in_ext,
                              row(mla_q_norm[j]), row(mla_kv_norm[j]), wn, wr, wrr, wuk, wuv_t,
                              batch=batch, heads=mla_heads, q_rank=q_rank, kv_rank=kv_rank,
                              qk_scale=qk_scale)
            o = attention(q.reshape(batch, seq, -1), k.reshape(batch, seq, -1), vt,
                          batch=batch, seq=seq, heads=mla_heads)
            o = o.reshape(t, -1)
            mixer_in = (o,)
            mixer_specs = (pl.BlockSpec((TAIL_ROWS, o.shape[1]), lambda r: (r, 0)),)
            w_mix, tail_kw = mla_w_o[j], {}
        consts = (w_mix.astype(BF16), row(post_mix_norm[i]), row(pre_mlp_norm[i]),
                  w_mlp_in[i].astype(BF16), w_mlp_out[i].astype(BF16),
                  row(post_mlp_norm[i]), w_ple_proj[i].astype(BF16),
                  w_ple_gate[i].astype(BF16), row(ple_norm[i]))
        h = layer_tail(mixer_in, mixer_specs, h, p[i].reshape(t, -1), consts,
                       tm=TAIL_ROWS, **tail_kw)
    return h.reshape(batch, seq, d)
```

```python
import functools
import math

import jax
import jax.numpy as jnp
from jax import lax
from jax.experimental import pallas as pl
from jax.experimental.pallas import tpu as pltpu

EPS = 1e-6
ROPE_BASE = 10000.0
LANE = 128
GLA_CHUNK = 64
VMEM_LIMIT = 56 * 1024 * 1024
TAIL_ROWS = 512
Q_BOUND_LANE = 64
K_NORM_LANE = 65
BOUND_MARGIN = 1.01
FAST_SOFTMAX_MAX_BOUND = 48.0

F32 = jnp.float32
BF16 = jnp.bfloat16


def _rms(x, gain):
    ms = jnp.mean(x * x, axis=-1, keepdims=True)
    return x * lax.rsqrt(ms + EPS) * gain


def _params(**kw):
    return pltpu.CompilerParams(vmem_limit_bytes=VMEM_LIMIT, **kw)


def _const_spec(shape):
    nd = len(shape)
    return pl.BlockSpec(shape, lambda *_: (0,) * nd, pipeline_mode=pl.Buffered(1))


def _norm_matmul_kernel(x_ref, g_ref, w_ref, o_ref, *, tn):
    xn = _rms(x_ref[...], g_ref[...]).astype(BF16)
    for c in range(w_ref.shape[1] // tn):
        cols = slice(c * tn, (c + 1) * tn)
        o_ref[:, cols] = jnp.dot(xn, w_ref[:, cols],
                                 preferred_element_type=F32).astype(o_ref.dtype)


def norm_matmul(x, gain, w, *, tm=512, tn=1024):
    t, d = x.shape
    n = w.shape[1]
    return pl.pallas_call(
        functools.partial(_norm_matmul_kernel, tn=tn),
        grid=(t // tm,),
        in_specs=[pl.BlockSpec((tm, d), lambda i: (i, 0)),
                  _const_spec((1, d)), _const_spec((d, n))],
        out_specs=pl.BlockSpec((tm, n), lambda i: (i, 0)),
        out_shape=jax.ShapeDtypeStruct((t, n), BF16),
        compiler_params=_params(dimension_semantics=("arbitrary",)),
        name="norm_matmul",
    )(x, gain, w)


def _gla_kernel(lbl_ref, qf_ref, zf_ref, vf_ref, qb_ref, zb_ref, vb_ref,
                of_ref, ob_ref, st_ref, *, layer, heads, kdim, nsub):
    c = GLA_CHUNK

    @pl.when(pl.program_id(1) == 0)
    def _():
        st_ref[...] = jnp.zeros_like(st_ref)

    lg = lbl_ref[...].astype(F32)
    e = jnp.exp(lg - jnp.max(lg, axis=0, keepdims=True))
    lb = (jnp.sum(e[:layer + 1], axis=0, keepdims=True)
          / jnp.sum(e, axis=0, keepdims=True))

    row = lax.broadcasted_iota(jnp.int32, (c, c), 0)
    col = lax.broadcasted_iota(jnp.int32, (c, c), 1)
    keeps = (col <= row, col >= row)
    scale = kdim ** -0.5
    nt_dims = (((1,), (1,)), ((), ()))
    refs = ((qf_ref, zf_ref, vf_ref, of_ref), (qb_ref, zb_ref, vb_ref, ob_ref))

    def rows_of(direction, s):
        return pl.ds((s if direction == 0 else nsub - 1 - s) * c, c)

    def prep(direction, s):
        q_ref, z_ref, _, _ = refs[direction]
        rows = rows_of(direction, s)
        f = lb + (1.0 - lb) * jax.nn.sigmoid(z_ref[rows, :].astype(F32))
        g = jnp.log2(f)
        g_hi = g.astype(BF16)
        g_lo = (g - g_hi.astype(F32)).astype(BF16)
        tri = keeps[direction].astype(BF16)
        gc = (jnp.dot(tri, g_hi, preferred_element_type=F32)
              + jnp.dot(tri, g_lo, preferred_element_type=F32))
        decay = jnp.exp2(gc[c - 1:c, :] if direction == 0 else gc[0:1, :])
        q_dec = (q_ref[rows, :].astype(F32) * scale * jnp.exp2(gc)).astype(BF16)
        k_inv = (1.0 - f) * jnp.exp2(-gc)
        k_end = (k_inv * decay).astype(BF16)
        return q_dec, k_inv.astype(BF16), k_end, decay

    def state_stage(direction, s, prepped):
        q_dec, k_inv, k_end, decay = prepped
        v = refs[direction][2][rows_of(direction, s), :]
        out = []
        for h in range(heads):
            sl = slice(h * kdim, (h + 1) * kdim)
            a = lax.dot_general(q_dec[:, sl], k_inv[:, sl], nt_dims,
                                preferred_element_type=F32)
            st = st_ref[direction, h]
            o_state = lax.dot_general(q_dec[:, sl], st.astype(BF16), nt_dims,
                                      preferred_element_type=F32)
            st_ref[direction, h] = st * decay[:, sl] + lax.dot_general(
                v[:, sl], k_end[:, sl], (((0,), (0,)), ((), ())),
                preferred_element_type=F32)
            out.append((a, o_state))
        return out

    def output_stage(direction, s, staged):
        _, _, v_ref, o_ref = refs[direction]
        rows = rows_of(direction, s)
        v = v_ref[rows, :]
        for h in range(heads):
            sl = slice(h * kdim, (h + 1) * kdim)
            a, o_state = staged[h]
            a = jnp.where(keeps[direction], a, 0.0).astype(BF16)
            o = jnp.dot(a, v[:, sl], preferred_element_type=F32) + o_state
            o_ref[rows, sl] = o.astype(o_ref.dtype)

    dirs = (0, 1)
    prepped = [prep(d, 0) for d in dirs]
    staged_prev = None
    for s in range(nsub):
        prepped_next = [prep(d, s + 1) for d in dirs] if s + 1 < nsub else None
        staged = [state_stage(d, s, prepped[d]) for d in dirs]
        if staged_prev is not None:
            for d in dirs:
                output_stage(d, s - 1, staged_prev[d])
        staged_prev, prepped = staged, prepped_next
    for d in dirs:
        output_stage(d, nsub - 1, staged_prev[d])


def gla_bidirectional(proj, lb_logits, *, layer, batch, seq, heads, kdim, nsub=8):
    hk = heads * kdim
    tt = nsub * GLA_CHUNK
    nt = seq // tt
    nl = lb_logits.shape[0]

    def fwd(col):
        return pl.BlockSpec((None, tt, hk), lambda b, t: (b, t, col))

    def bwd(col):
        return pl.BlockSpec((None, tt, hk), lambda b, t: (b, nt - 1 - t, col))

    kern = functools.partial(_gla_kernel, layer=layer, heads=heads, kdim=kdim, nsub=nsub)
    return pl.pallas_call(
        kern,
        grid=(batch, nt),
        in_specs=[pl.BlockSpec((nl, hk), lambda b, t: (0, 0)),
                  fwd(0), fwd(1), fwd(3), bwd(0), bwd(2), bwd(3)],
        out_specs=[pl.BlockSpec((None, tt, hk), lambda b, t: (b, t, 0)),
                   pl.BlockSpec((None, tt, hk), lambda b, t: (b, nt - 1 - t, 0))],
        out_shape=[jax.ShapeDtypeStruct((batch, seq, hk), BF16)] * 2,
        scratch_shapes=[pltpu.VMEM((2, heads, kdim, kdim), F32)],
        compiler_params=_params(dimension_semantics=("arbitrary", "arbitrary")),
        name="gla_bidirectional",
    )(lb_logits, proj, proj, proj, proj, proj, proj)


def _hgrn_gated(of_ref, ob_ref, gate_ref, on_ref, *, heads, vdim):
    o = of_ref[...].astype(F32) + ob_ref[...].astype(F32)
    parts = []
    for h in range(heads):
        oh = o[:, h * vdim:(h + 1) * vdim]
        ms = jnp.mean(oh * oh, axis=-1, keepdims=True)
        parts.append(oh * lax.rsqrt(ms + EPS))
    on = jnp.concatenate(parts, axis=-1) * on_ref[...]
    gate = gate_ref[...].astype(F32)
    return (on * (gate * jax.nn.sigmoid(gate))).astype(BF16)


def _layer_tail_kernel(*refs, hgrn_heads, hgrn_vdim, tf):
    if hgrn_heads:
        of_ref, ob_ref, gate_ref, on_ref = refs[:4]
        y = _hgrn_gated(of_ref, ob_ref, gate_ref, on_ref, heads=hgrn_heads, vdim=hgrn_vdim)
        refs = refs[4:]
    else:
        y = refs[0][...]
        refs = refs[1:]
    (h_ref, p_ref, wm_ref, gm_ref, g1_ref, w1_ref, w2_ref, g2_ref, wp_ref, wg_ref, g3_ref,
     o_ref) = refs
    h = h_ref[...] + _rms(jnp.dot(y, wm_ref[...], preferred_element_type=F32), gm_ref[...])
    a = _rms(h, g1_ref[...]).astype(BF16)
    dff = w1_ref.shape[1]
    acc = jnp.zeros(h.shape, F32)
    for c in range(dff // tf):
        u = jnp.dot(a, w1_ref[:, c * tf:(c + 1) * tf], preferred_element_type=F32)
        u = jnp.maximum(u, 0.0)
        acc = acc + jnp.dot((u * u).astype(BF16), w2_ref[c * tf:(c + 1) * tf, :],
                            preferred_element_type=F32)
    h2 = h + _rms(acc, g2_ref[...])
    e = jnp.dot(p_ref[...].astype(BF16), wp_ref[...], preferred_element_type=F32)
    gt = jax.nn.sigmoid(jnp.dot(h2.astype(BF16), wg_ref[...], preferred_element_type=F32))
    o_ref[...] = h2 + _rms(gt * e, g3_ref[...])


def layer_tail(mixer_in, mixer_specs, h, p, consts, *, hgrn_heads=0, hgrn_vdim=0,
               tm=512, tf=1024):
    t, d = h.shape
    pd = p.shape[1]
    row = lambda i: (i, 0)
    kern = functools.partial(_layer_tail_kernel, hgrn_heads=hgrn_heads, hgrn_vdim=hgrn_vdim,
                             tf=tf)
    return pl.pallas_call(
        kern,
        grid=(t // tm,),
        in_specs=list(mixer_specs)
                 + [pl.BlockSpec((tm, d), row), pl.BlockSpec((tm, pd), row)]
                 + [_const_spec(c.shape) for c in consts],
        out_specs=pl.BlockSpec((tm, d), row),
        out_shape=jax.ShapeDtypeStruct((t, d), F32),
        compiler_params=_params(dimension_semantics=("arbitrary",)),
        name="layer_tail",
    )(*mixer_in, h, p, *consts)


def _mla_in_kernel(h_ref, pos_ref, inv_ref, g_ref, win_ref, qg_ref, kvg_ref,
                   wn_ref, wr_ref, wrr_ref, wuk_ref, wuv_ref,
                   en_ref, er_ref, ones_ref,
                   q_ref, k_ref, v_ref, *, heads, q_rank, kv_rank, qk_scale):
    a = _rms(h_ref[...], g_ref[...]).astype(BF16)
    proj = jnp.dot(a, win_ref[...], preferred_element_type=F32)
    cq = _rms(proj[:, :q_rank], qg_ref[...]).astype(BF16)
    c0 = q_rank + kv_rank
    ckv = _rms(proj[:, q_rank:c0], kvg_ref[...]).astype(BF16)
    ang = pos_ref[...].astype(F32) * inv_ref[...]
    cos = jnp.cos(ang)
    sin = jnp.sin(ang)
    lane = lax.broadcasted_iota(jnp.int32, (1, LANE), 1)
    low_half = lane < LANE // 2
    kr = jnp.where(low_half,
                   proj[:, c0:c0 + LANE] * cos + proj[:, c0 + LANE:c0 + 2 * LANE] * sin, 0.0)
    kr_ones = jnp.where(lane == Q_BOUND_LANE, 1.0, kr)
    qn = jnp.dot(cq, wn_ref[...], preferred_element_type=F32) * qk_scale
    cos_t = jnp.tile(cos * qk_scale, (1, heads // 2))
    sin_t = jnp.tile(sin * qk_scale, (1, heads // 2))
    qr = (jnp.dot(cq, wr_ref[...], preferred_element_type=F32) * cos_t
          + jnp.dot(cq, wrr_ref[...], preferred_element_type=F32) * sin_t)
    kn = jnp.dot(ckv, wuk_ref[...], preferred_element_type=F32)

    def sq(x):
        return (x * x).astype(BF16)

    def mm(x, w_ref):
        return jnp.dot(x, w_ref[...], preferred_element_type=F32)

    q_bound = -BOUND_MARGIN * jnp.sqrt(mm(sq(qn), en_ref) + mm(sq(qr), er_ref))
    k_norm = BOUND_MARGIN * jnp.sqrt(mm(sq(kn), en_ref) + mm(sq(kr), ones_ref))
    for h in range(heads):
        sl = slice(h * LANE, (h + 1) * LANE)
        pair = qr[:, (h // 2) * LANE:(h // 2 + 1) * LANE]
        if h % 2:
            pair = pltpu.roll(pair, LANE // 2, axis=1)
        q_lane = pltpu.roll(q_bound, (Q_BOUND_LANE - h) % LANE, axis=1)
        k_lane = pltpu.roll(k_norm, (K_NORM_LANE - h) % LANE, axis=1)
        q_ref[:, 2 * h * LANE:(2 * h + 1) * LANE] = qn[:, sl].astype(BF16)
        q_ref[:, (2 * h + 1) * LANE:(2 * h + 2) * LANE] = jnp.where(
            lane == Q_BOUND_LANE, q_lane, jnp.where(low_half, pair, 0.0)).astype(BF16)
        k_ref[:, 2 * h * LANE:(2 * h + 1) * LANE] = kn[:, sl].astype(BF16)
        k_ref[:, (2 * h + 1) * LANE:(2 * h + 2) * LANE] = jnp.where(
            lane == K_NORM_LANE, k_lane, kr_ones).astype(BF16)
    v_ref[...] = lax.dot_general(wuv_ref[...], ckv, (((1,), (1,)), ((), ())),
                                 preferred_element_type=F32).astype(BF16)


def mla_in(h, pos, inv, gain, w_in, q_gain, kv_gain, wn, wr, wrr, wuk, wuv_t,
           *, batch, heads, q_rank, kv_rank, qk_scale, tm=256):
    t, d = h.shape
    nsb = t // batch // tm
    row = lambda i: (i, 0)
    kern = functools.partial(_mla_in_kernel, heads=heads, q_rank=q_rank, kv_rank=kv_rank,
                             qk_scale=qk_scale)
    col = jnp.arange(LANE)[None, :]
    e_nope = (jnp.arange(heads * LANE)[:, None] // LANE == col).astype(BF16)
    e_rope = (jnp.arange(heads * LANE // 2)[:, None] // (LANE // 2) == col).astype(BF16)
    ones = jnp.ones((LANE, LANE), BF16)
    consts = [inv, gain, w_in, q_gain, kv_gain, wn, wr, wrr, wuk, wuv_t, e_nope, e_rope, ones]
    return pl.pallas_call(
        kern,
        grid=(t // tm,),
        in_specs=[pl.BlockSpec((tm, d), row), pl.BlockSpec((tm, 1), row)]
                 + [_const_spec(c.shape) for c in consts],
        out_specs=[pl.BlockSpec((tm, 2 * heads * LANE), row),
                   pl.BlockSpec((tm, 2 * heads * LANE), row),
                   pl.BlockSpec((None, heads * LANE, tm), lambda i: (i // nsb, 0, i % nsb))],
        out_shape=[jax.ShapeDtypeStruct((t, 2 * heads * LANE), BF16),
                   jax.ShapeDtypeStruct((t, 2 * heads * LANE), BF16),
                   jax.ShapeDtypeStruct((batch, heads * LANE, t // batch), BF16)],
        compiler_params=_params(dimension_semantics=("arbitrary",)),
        name="mla_in",
    )(h, pos, *consts)


def _attn_stream(q_tiles, k_ref, vt_ref, o_ref, *, tq, tk, ahead, online_max):
    nk = k_ref.shape[0] // tk
    total = len(q_tiles) * nk

    def scores(t):
        i, j = divmod(t, nk)
        return lax.dot_general(k_ref[j * tk:(j + 1) * tk, :], q_tiles[i],
                               (((1,), (1,)), ((), ())),
                               preferred_element_type=F32)

    pending = [scores(t) for t in range(min(ahead, total))]
    for t in range(total):
        i, j = divmod(t, nk)
        s = pending.pop(0)
        if t + ahead < total:
            pending.append(scores(t + ahead))
        if online_max:
            s_max = jnp.max(s, axis=0, keepdims=True)
            m_new = s_max if j == 0 else jnp.maximum(m, s_max)
            s = s - m_new
        p = jnp.exp2(s)
        p_sum = jnp.sum(p, axis=0, keepdims=True)
        pv = jnp.dot(vt_ref[:, j * tk:(j + 1) * tk], p.astype(BF16),
                     preferred_element_type=F32)
        if j == 0:
            l, acc = p_sum, pv
        elif online_max:
            alpha = jnp.exp2(m - m_new)
            l = alpha * l + p_sum
            acc = alpha * acc + pv
        else:
            l = l + p_sum
            acc = acc + pv
        if online_max:
            m = m_new
        if j == nk - 1:
            o_ref[i * tq:(i + 1) * tq, :] = (acc / l).T.astype(o_ref.dtype)


def _attn_kernel(q_ref, k_ref, vt_ref, o_ref, *, tq, tk, ahead):
    nq = q_ref.shape[0] // tq
    lane = lax.broadcasted_iota(jnp.int32, (1, LANE), 1)
    bound_lane = lane == Q_BOUND_LANE

    def over_rows(x, op):
        while x.shape[0] > 16:
            half = x.shape[0] // 2
            x = op(x[:half], x[half:])
        return x.astype(F32)

    k_top = jnp.max(over_rows(k_ref[:, LANE:2 * LANE], jnp.maximum), axis=0, keepdims=True)
    k_max = jnp.max(jnp.where(lane == K_NORM_LANE, k_top, 0.0), axis=1, keepdims=True)
    q_low = jnp.min(over_rows(q_ref[:, LANE:2 * LANE], jnp.minimum), axis=0, keepdims=True)
    q_max = jnp.max(jnp.where(bound_lane, -q_low, 0.0), axis=1, keepdims=True)
    worst = jnp.max(q_max * k_max)
    lane_scale = jnp.where(bound_lane, k_max, 1.0).astype(BF16)
    kw = dict(tq=tq, tk=tk, ahead=ahead)

    def q_tiles(fix_rope):
        return [jnp.concatenate([q_ref[i * tq:(i + 1) * tq, :LANE],
                                 fix_rope(q_ref[i * tq:(i + 1) * tq, LANE:2 * LANE])], axis=1)
                for i in range(nq)]

    @pl.when(worst <= FAST_SOFTMAX_MAX_BOUND)
    def _():
        tiles = q_tiles(lambda r: r * lane_scale)
        _attn_stream(tiles, k_ref, vt_ref, o_ref, online_max=False, **kw)

    @pl.when(jnp.logical_not(worst <= FAST_SOFTMAX_MAX_BOUND))
    def _():
        tiles = q_tiles(lambda r: jnp.where(bound_lane, jnp.zeros((), BF16), r))
        _attn_stream(tiles, k_ref, vt_ref, o_ref, online_max=True, **kw)


def attention(q, k, vt, *, batch, seq, heads, tq=512, nq=4, tk=1024, ahead=2):
    dq = q.shape[-1] // heads
    dv = vt.shape[1] // heads
    tqb = tq * nq
    return pl.pallas_call(
        functools.partial(_attn_kernel, tq=tq, tk=tk, ahead=ahead),
        grid=(batch, heads, seq // tqb),
        in_specs=[pl.BlockSpec((None, tqb, dq), lambda b, h, i: (b, i, h)),
                  pl.BlockSpec((None, seq, dq), lambda b, h, i: (b, 0, h)),
                  pl.BlockSpec((None, dv, seq), lambda b, h, i: (b, h, 0))],
        out_specs=pl.BlockSpec((None, tqb, dv), lambda b, h, i: (b, i, h)),
        out_shape=jax.ShapeDtypeStruct((batch, seq, heads * dv), BF16),
        compiler_params=_params(dimension_semantics=("arbitrary",) * 3),
        name="attention",
    )(q, k, vt)


def _rot_half_cols(w):
    half = w.shape[-1] // 2
    return jnp.concatenate([-w[..., half:], w[..., :half]], axis=-1)


def kernel(x, p, positions, pre_mix_norm, post_mix_norm, pre_mlp_norm, post_mlp_norm,
           w_mlp_in, w_mlp_out, w_ple_proj, w_ple_gate, ple_norm,
           hg_lb_logits, hg_w_in, hg_o_norm, hg_w_out,
           mla_w_in, mla_q_norm, mla_w_uq, mla_kv_norm, mla_w_ukv, mla_w_o):
    batch, seq, d = x.shape
    depth = p.shape[0]
    t = batch * seq
    hg_heads, hg_vdim = hg_o_norm.shape[1], hg_o_norm.shape[2]
    hg_kdim = hg_lb_logits.shape[1] // hg_heads
    q_rank = mla_q_norm.shape[1]
    kv_rank = mla_kv_norm.shape[1]
    rope = mla_w_in.shape[2] - q_rank - kv_rank
    mla_heads = mla_w_o.shape[1] // LANE
    nope = mla_w_uq.shape[2] // mla_heads - rope
    vdim = mla_w_ukv.shape[2] // mla_heads - nope
    assert nope == LANE and vdim == LANE and 2 * rope == LANE

    row = lambda g: g.reshape(1, -1).astype(F32)
    h = x.reshape(t, d)
    pos = positions.reshape(t, 1)
    inv = 1.0 / (ROPE_BASE ** (jnp.arange(0, rope, 2, dtype=F32) / rope))
    inv = jnp.tile(inv, LANE // inv.shape[0]).reshape(1, LANE)
    qk_scale = (nope + rope) ** -0.5 * math.log2(math.e)

    for i in range(depth):
        j = i // 2
        if i % 2 == 0:
            proj = norm_matmul(h, row(pre_mix_norm[i]), hg_w_in[j].astype(BF16))
            o_f, o_b = gla_bidirectional(
                proj.reshape(batch, seq, -1), hg_lb_logits.astype(F32), layer=i,
                batch=batch, seq=seq, heads=hg_heads, kdim=hg_kdim)
            hv = hg_heads * hg_vdim
            tile = lambda col: pl.BlockSpec((TAIL_ROWS, hv), lambda r: (r, col))
            mixer_in = (o_f.reshape(t, -1), o_b.reshape(t, -1), proj, row(hg_o_norm[j]))
            mixer_specs = (tile(0), tile(0), tile(4), _const_spec((1, hv)))
            w_mix, tail_kw = hg_w_out[j], dict(hgrn_heads=hg_heads, hgrn_vdim=hg_vdim)
        else:
            w_in = mla_w_in[j]
            c0 = q_rank + kv_rank
            w_kr = w_in[:, c0:]
            w_krr = _rot_half_cols(w_kr)
            w_in_ext = jnp.concatenate([w_in[:, :c0], w_kr, w_kr, w_krr, w_krr],
                                       axis=-1).astype(BF16)
            w_uq = mla_w_uq[j].reshape(q_rank, mla_heads, nope + rope)
            wn = w_uq[..., :nope].reshape(q_rank, -1).astype(BF16)
            w_qr = w_uq[..., nope:]
            wr = w_qr.reshape(q_rank, -1).astype(BF16)
            wrr = _rot_half_cols(w_qr).reshape(q_rank, -1).astype(BF16)
            w_ukv = mla_w_ukv[j].reshape(kv_rank, mla_heads, nope + vdim)
            wuk = w_ukv[..., :nope].reshape(kv_rank, -1).astype(BF16)
            wuv_t = w_ukv[..., nope:].reshape(kv_rank, -1).T.astype(BF16)
            q, k, vt = mla_in(h, pos, inv, row(pre_mix_norm[i]), w_in_ext,
                              row(mla_q_norm[j]), row(mla_kv_norm[j]), wn, wr, wrr, wuk, wuv_t,
                              batch=batch, heads=mla_heads, q_rank=q_rank, kv_rank=kv_rank,
                              qk_scale=qk_scale)
            o = attention(q.reshape(batch, seq, -1), k.reshape(batch, seq, -1), vt,
                          batch=batch, seq=seq, heads=mla_heads)
            o = o.reshape(t, -1)
            mixer_in = (o,)
            mixer_specs = (pl.BlockSpec((TAIL_ROWS, o.shape[1]), lambda r: (r, 0)),)
            w_mix, tail_kw = mla_w_o[j], {}
        consts = (w_mix.astype(BF16), row(post_mix_norm[i]), row(pre_mlp_norm[i]),
                  w_mlp_in[i].astype(BF16), w_mlp_out[i].astype(BF16),
                  row(post_mlp_norm[i]), w_ple_proj[i].astype(BF16),
                  w_ple_gate[i].astype(BF16), row(ple_norm[i]))
        h = layer_tail(mixer_in, mixer_specs, h, p[i].reshape(t, -1), consts,
                       tm=TAIL_ROWS, **tail_kw)
    return h.reshape(batch, seq, d)
```

```python
import functools
import math

import jax
import jax.numpy as jnp
from jax import lax
from jax.experimental import pallas as pl
from jax.experimental.pallas import tpu as pltpu

EPS = 1e-6
ROPE_BASE = 10000.0
LANE = 128
GLA_CHUNK = 64
VMEM_LIMIT = 56 * 1024 * 1024
TAIL_ROWS = 512
Q_BOUND_LANE = 64
K_NORM_LANE = 65
BOUND_MARGIN = 1.01
FAST_SOFTMAX_MAX_BOUND = 48.0

F32 = jnp.float32
BF16 = jnp.bfloat16


def _rms(x, gain):
    ms = jnp.mean(x * x, axis=-1, keepdims=True)
    return x * lax.rsqrt(ms + EPS) * gain


def _params(**kw):
    return pltpu.CompilerParams(vmem_limit_bytes=VMEM_LIMIT, **kw)


def _const_spec(shape):
    nd = len(shape)
    return pl.BlockSpec(shape, lambda *_: (0,) * nd, pipeline_mode=pl.Buffered(1))


def _norm_matmul_kernel(x_ref, g_ref, w_ref, o_ref, *, tn):
    xn = _rms(x_ref[...], g_ref[...]).astype(BF16)
    for c in range(w_ref.shape[1] // tn):
        cols = slice(c * tn, (c + 1) * tn)
        o_ref[:, cols] = jnp.dot(xn, w_ref[:, cols],
                                 preferred_element_type=F32).astype(o_ref.dtype)


def norm_matmul(x, gain, w, *, tm=512, tn=1024):
    t, d = x.shape
    n = w.shape[1]
    return pl.pallas_call(
        functools.partial(_norm_matmul_kernel, tn=tn),
        grid=(t // tm,),
        in_specs=[pl.BlockSpec((tm, d), lambda i: (i, 0)),
                  _const_spec((1, d)), _const_spec((d, n))],
        out_specs=pl.BlockSpec((tm, n), lambda i: (i, 0)),
        out_shape=jax.ShapeDtypeStruct((t, n), BF16),
        compiler_params=_params(dimension_semantics=("arbitrary",)),
        name="norm_matmul",
    )(x, gain, w)


def _gla_kernel(lbl_ref, qf_ref, zf_ref, vf_ref, qb_ref, zb_ref, vb_ref,
                of_ref, ob_ref, st_ref, *, layer, heads, kdim, nsub):
    c = GLA_CHUNK

    @pl.when(pl.program_id(1) == 0)
    def _():
        st_ref[...] = jnp.zeros_like(st_ref)

    lg = lbl_ref[...].astype(F32)
    e = jnp.exp(lg - jnp.max(lg, axis=0, keepdims=True))
    lb = (jnp.sum(e[:layer + 1], axis=0, keepdims=True)
          / jnp.sum(e, axis=0, keepdims=True))

    row = lax.broadcasted_iota(jnp.int32, (c, c), 0)
    col = lax.broadcasted_iota(jnp.int32, (c, c), 1)
    keeps = (col <= row, col >= row)
    scale = kdim ** -0.5
    nt_dims = (((1,), (1,)), ((), ()))
    refs = ((qf_ref, zf_ref, vf_ref, of_ref), (qb_ref, zb_ref, vb_ref, ob_ref))

    def rows_of(direction, s):
        return pl.ds((s if direction == 0 else nsub - 1 - s) * c, c)

    def prep(direction, s):
        q_ref, z_ref, _, _ = refs[direction]
        rows = rows_of(direction, s)
        f = lb + (1.0 - lb) * jax.nn.sigmoid(z_ref[rows, :].astype(F32))
        g = jnp.log2(f)
        g_hi = g.astype(BF16)
        g_lo = (g - g_hi.astype(F32)).astype(BF16)
        tri = keeps[direction].astype(BF16)
        gc = (jnp.dot(tri, g_hi, preferred_element_type=F32)
              + jnp.dot(tri, g_lo, preferred_element_type=F32))
        decay = jnp.exp2(gc[c - 1:c, :] if direction == 0 else gc[0:1, :])
        q_dec = (q_ref[rows, :].astype(F32) * scale * jnp.exp2(gc)).astype(BF16)
        k_inv = (1.0 - f) * jnp.exp2(-gc)
        k_end = (k_inv * decay).astype(BF16)
        return q_dec, k_inv.astype(BF16), k_end, decay

    def state_stage(direction, s, prepped):
        q_dec, k_inv, k_end, decay = prepped
        v = refs[direction][2][rows_of(direction, s), :]
        out = []
        for h in range(heads):
            sl = slice(h * kdim, (h + 1) * kdim)
            a = lax.dot_general(q_dec[:, sl], k_inv[:, sl], nt_dims,
                                preferred_element_type=F32)
            st = st_ref[direction, h]
            o_state = lax.dot_general(q_dec[:, sl], st.astype(BF16), nt_dims,
                                      preferred_element_type=F32)
            st_ref[direction, h] = st * decay[:, sl] + lax.dot_general(
                v[:, sl], k_end[:, sl], (((0,), (0,)), ((), ())),
                preferred_element_type=F32)
            out.append((a, o_state))
        return out

    def output_stage(direction, s, staged):
        _, _, v_ref, o_ref = refs[direction]
        rows = rows_of(direction, s)
        v = v_ref[rows, :]
        for h in range(heads):
            sl = slice(h * kdim, (h + 1) * kdim)
            a, o_state = staged[h]
            a = jnp.where(keeps[direction], a, 0.0).astype(BF16)
            o = jnp.dot(a, v[:, sl], preferred_element_type=F32) + o_state
            o_ref[rows, sl] = o.astype(o_ref.dtype)

    dirs = (0, 1)
    prepped = [prep(d, 0) for d in dirs]
    staged_prev = None
    for s in range(nsub):
        prepped_next = [prep(d, s + 1) for d in dirs] if s + 1 < nsub else None
        staged = [state_stage(d, s, prepped[d]) for d in dirs]
        if staged_prev is not None:
            for d in dirs:
                output_stage(d, s - 1, staged_prev[d])
        staged_prev, prepped = staged, prepped_next
    for d in dirs:
        output_stage(d, nsub - 1, staged_prev[d])


def gla_bidirectional(proj, lb_logits, *, layer, batch, seq, heads, kdim, nsub=8):
    hk = heads * kdim
    tt = nsub * GLA_CHUNK
    nt = seq // tt
    nl = lb_logits.shape[0]

    def fwd(col):
        return pl.BlockSpec((None, tt, hk), lambda b, t: (b, t, col))

    def bwd(col):
        return pl.BlockSpec((None, tt, hk), lambda b, t: (b, nt - 1 - t, col))

    kern = functools.partial(_gla_kernel, layer=layer, heads=heads, kdim=kdim, nsub=nsub)
    return pl.pallas_call(
        kern,
        grid=(batch, nt),
        in_specs=[pl.BlockSpec((nl, hk), lambda b, t: (0, 0)),
                  fwd(0), fwd(1), fwd(3), bwd(0), bwd(2), bwd(3)],
        out_specs=[pl.BlockSpec((None, tt, hk), lambda b, t: (b, t, 0)),
                   pl.BlockSpec((None, tt, hk), lambda b, t: (b, nt - 1 - t, 0))],
        out_shape=[jax.ShapeDtypeStruct((batch, seq, hk), BF16)] * 2,
        scratch_shapes=[pltpu.VMEM((2, heads, kdim, kdim), F32)],
        compiler_params=_params(dimension_semantics=("arbitrary", "arbitrary")),
        name="gla_bidirectional",
    )(lb_logits, proj, proj, proj, proj, proj, proj)


def _hgrn_gated(of_ref, ob_ref, gate_ref, on_ref, rows, *, heads, vdim):
    o = of_ref[rows, :].astype(F32) + ob_ref[rows, :].astype(F32)
    parts = []
    for h in range(heads):
        oh = o[:, h * vdim:(h + 1) * vdim]
        ms = jnp.mean(oh * oh, axis=-1, keepdims=True)
        parts.append(oh * lax.rsqrt(ms + EPS))
    on = jnp.concatenate(parts, axis=-1) * on_ref[...]
    gate = gate_ref[rows, :].astype(F32)
    return (on * (gate * jax.nn.sigmoid(gate))).astype(BF16)


def _layer_tail_kernel(*refs, hgrn_heads, hgrn_vdim, tf, splits):
    if hgrn_heads:
        of_ref, ob_ref, gate_ref, on_ref = refs[:4]
        mixer_out = lambda rows: _hgrn_gated(of_ref, ob_ref, gate_ref, on_ref, rows,
                                             heads=hgrn_heads, vdim=hgrn_vdim)
        refs = refs[4:]
    else:
        mixer_out = lambda rows: refs[0][rows, :]
    (h_ref, p_ref, wm_ref, gm_ref, g1_ref, w1_ref, w2_ref, g2_ref, wp_ref, wg_ref, g3_ref,
     o_ref) = refs[-12:]
    dff = w1_ref.shape[1]
    rs = h_ref.shape[0] // splits

    def mm(x, w):
        return jnp.dot(x, w, preferred_element_type=F32)

    def group(g):
        rows = slice(g * rs, (g + 1) * rs)
        y = mixer_out(rows)
        h = h_ref[rows, :] + _rms(mm(y, wm_ref[...]), gm_ref[...])
        a = _rms(h, g1_ref[...]).astype(BF16)
        yield
        acc = None
        for c in range(dff // tf):
            u = jnp.maximum(mm(a, w1_ref[:, c * tf:(c + 1) * tf]), 0.0)
            part = mm((u * u).astype(BF16), w2_ref[c * tf:(c + 1) * tf, :])
            acc = part if acc is None else acc + part
        yield
        h2 = h + _rms(acc, g2_ref[...])
        e = mm(p_ref[rows, :].astype(BF16), wp_ref[...])
        gt = jax.nn.sigmoid(mm(h2.astype(BF16), wg_ref[...]))
        o_ref[rows, :] = h2 + _rms(gt * e, g3_ref[...])

    live = [group(g) for g in range(splits)]
    while live:
        for gen in list(live):
            if next(gen, StopIteration) is StopIteration:
                live.remove(gen)


def layer_tail(mixer_in, mixer_specs, h, p, consts, *, hgrn_heads=0, hgrn_vdim=0,
               tm=512, tf=1024, splits=2):
    t, d = h.shape
    pd = p.shape[1]
    row = lambda i: (i, 0)
    kern = functools.partial(_layer_tail_kernel, hgrn_heads=hgrn_heads, hgrn_vdim=hgrn_vdim,
                             tf=tf, splits=splits)
    return pl.pallas_call(
        kern,
        grid=(t // tm,),
        in_specs=list(mixer_specs)
                 + [pl.BlockSpec((tm, d), row), pl.BlockSpec((tm, pd), row)]
                 + [_const_spec(c.shape) for c in consts],
        out_specs=pl.BlockSpec((tm, d), row),
        out_shape=jax.ShapeDtypeStruct((t, d), F32),
        compiler_params=_params(dimension_semantics=("arbitrary",)),
        name="layer_tail",
    )(*mixer_in, h, p, *consts)


def _mla_in_kernel(h_ref, pos_ref, inv_ref, g_ref, win_ref, qg_ref, kvg_ref,
                   wn_ref, wr_ref, wrr_ref, wuk_ref, wuv_ref,
                   en_ref, er_ref, ones_ref,
                   q_ref, k_ref, v_ref, *, heads, q_rank, kv_rank, qk_scale):
    a = _rms(h_ref[...], g_ref[...]).astype(BF16)
    proj = jnp.dot(a, win_ref[...], preferred_element_type=F32)
    cq = _rms(proj[:, :q_rank], qg_ref[...]).astype(BF16)
    c0 = q_rank + kv_rank
    ckv = _rms(proj[:, q_rank:c0], kvg_ref[...]).astype(BF16)
    ang = pos_ref[...].astype(F32) * inv_ref[...]
    cos = jnp.cos(ang)
    sin = jnp.sin(ang)
    lane = lax.broadcasted_iota(jnp.int32, (1, LANE), 1)
    low_half = lane < LANE // 2
    kr = jnp.where(low_half,
                   proj[:, c0:c0 + LANE] * cos + proj[:, c0 + LANE:c0 + 2 * LANE] * sin, 0.0)
    kr_ones = jnp.where(lane == Q_BOUND_LANE, 1.0, kr)
    qn = jnp.dot(cq, wn_ref[...], preferred_element_type=F32) * qk_scale
    cos_t = jnp.tile(cos * qk_scale, (1, heads // 2))
    sin_t = jnp.tile(sin * qk_scale, (1, heads // 2))
    qr = (jnp.dot(cq, wr_ref[...], preferred_element_type=F32) * cos_t
          + jnp.dot(cq, wrr_ref[...], preferred_element_type=F32) * sin_t)
    kn = jnp.dot(ckv, wuk_ref[...], preferred_element_type=F32)

    def sq(x):
        return (x * x).astype(BF16)

    def mm(x, w_ref):
        return jnp.dot(x, w_ref[...], preferred_element_type=F32)

    q_bound = -BOUND_MARGIN * jnp.sqrt(mm(sq(qn), en_ref) + mm(sq(qr), er_ref))
    k_norm = BOUND_MARGIN * jnp.sqrt(mm(sq(kn), en_ref) + mm(sq(kr), ones_ref))
    for h in range(heads):
        sl = slice(h * LANE, (h + 1) * LANE)
        pair = qr[:, (h // 2) * LANE:(h // 2 + 1) * LANE]
        if h % 2:
            pair = pltpu.roll(pair, LANE // 2, axis=1)
        q_lane = pltpu.roll(q_bound, (Q_BOUND_LANE - h) % LANE, axis=1)
        k_lane = pltpu.roll(k_norm, (K_NORM_LANE - h) % LANE, axis=1)
        q_ref[:, 2 * h * LANE:(2 * h + 1) * LANE] = qn[:, sl].astype(BF16)
        q_ref[:, (2 * h + 1) * LANE:(2 * h + 2) * LANE] = jnp.where(
            lane == Q_BOUND_LANE, q_lane, jnp.where(low_half, pair, 0.0)).astype(BF16)
        k_ref[:, 2 * h * LANE:(2 * h + 1) * LANE] = kn[:, sl].astype(BF16)
        k_ref[:, (2 * h + 1) * LANE:(2 * h + 2) * LANE] = jnp.where(
            lane == K_NORM_LANE, k_lane, kr_ones).astype(BF16)
    v_ref[...] = lax.dot_general(wuv_ref[...], ckv, (((1,), (1,)), ((), ())),
                                 preferred_element_type=F32).astype(BF16)


def mla_in(h, pos, inv, gain, w_in, q_gain, kv_gain, wn, wr, wrr, wuk, wuv_t,
           *, batch, heads, q_rank, kv_rank, qk_scale, tm=256):
    t, d = h.shape
    nsb = t // batch // tm
    row = lambda i: (i, 0)
    kern = functools.partial(_mla_in_kernel, heads=heads, q_rank=q_rank, kv_rank=kv_rank,
                             qk_scale=qk_scale)
    col = jnp.arange(LANE)[None, :]
    e_nope = (jnp.arange(heads * LANE)[:, None] // LANE == col).astype(BF16)
    e_rope = (jnp.arange(heads * LANE // 2)[:, None] // (LANE // 2) == col).astype(BF16)
    ones = jnp.ones((LANE, LANE), BF16)
    consts = [inv, gain, w_in, q_gain, kv_gain, wn, wr, wrr, wuk, wuv_t, e_nope, e_rope, ones]
    return pl.pallas_call(
        kern,
        grid=(t // tm,),
        in_specs=[pl.BlockSpec((tm, d), row), pl.BlockSpec((tm, 1), row)]
                 + [_const_spec(c.shape) for c in consts],
        out_specs=[pl.BlockSpec((tm, 2 * heads * LANE), row),
                   pl.BlockSpec((tm, 2 * heads * LANE), row),
                   pl.BlockSpec((None, heads * LANE, tm), lambda i: (i // nsb, 0, i % nsb))],
        out_shape=[jax.ShapeDtypeStruct((t, 2 * heads * LANE), BF16),
                   jax.ShapeDtypeStruct((t, 2 * heads * LANE), BF16),
                   jax.ShapeDtypeStruct((batch, heads * LANE, t // batch), BF16)],
        compiler_params=_params(dimension_semantics=("arbitrary",)),
        name="mla_in",
    )(h, pos, *consts)


def _attn_stream(q_tiles, k_ref, vt_ref, o_ref, *, tq, tk, ahead, online_max):
    nk = k_ref.shape[0] // tk
    total = len(q_tiles) * nk

    def scores(t):
        i, j = divmod(t, nk)
        return lax.dot_general(k_ref[j * tk:(j + 1) * tk, :], q_tiles[i],
                               (((1,), (1,)), ((), ())),
                               preferred_element_type=F32)

    pending = [scores(t) for t in range(min(ahead, total))]
    for t in range(total):
        i, j = divmod(t, nk)
        s = pending.pop(0)
        if t + ahead < total:
            pending.append(scores(t + ahead))
        if online_max:
            s_max = jnp.max(s, axis=0, keepdims=True)
            m_new = s_max if j == 0 else jnp.maximum(m, s_max)
            s = s - m_new
        p = jnp.exp2(s)
        p_sum = jnp.sum(p, axis=0, keepdims=True)
        pv = jnp.dot(vt_ref[:, j * tk:(j + 1) * tk], p.astype(BF16),
                     preferred_element_type=F32)
        if j == 0:
            l, acc = p_sum, pv
        elif online_max:
            alpha = jnp.exp2(m - m_new)
            l = alpha * l + p_sum
            acc = alpha * acc + pv
        else:
            l = l + p_sum
            acc = acc + pv
        if online_max:
            m = m_new
        if j == nk - 1:
            o_ref[i * tq:(i + 1) * tq, :] = (acc / l).T.astype(o_ref.dtype)


def _attn_kernel(q_ref, k_ref, vt_ref, o_ref, *, tq, tk, ahead):
    nq = q_ref.shape[0] // tq
    lane = lax.broadcasted_iota(jnp.int32, (1, LANE), 1)
    bound_lane = lane == Q_BOUND_LANE

    def over_rows(x, op):
        while x.shape[0] > 16:
            half = x.shape[0] // 2
            x = op(x[:half], x[half:])
        return x.astype(F32)

    k_top = jnp.max(over_rows(k_ref[:, LANE:2 * LANE], jnp.maximum), axis=0, keepdims=True)
    k_max = jnp.max(jnp.where(lane == K_NORM_LANE, k_top, 0.0), axis=1, keepdims=True)
    q_low = jnp.min(over_rows(q_ref[:, LANE:2 * LANE], jnp.minimum), axis=0, keepdims=True)
    q_max = jnp.max(jnp.where(bound_lane, -q_low, 0.0), axis=1, keepdims=True)
    worst = jnp.max(q_max * k_max)
    lane_scale = jnp.where(bound_lane, k_max, 1.0).astype(BF16)
    kw = dict(tq=tq, tk=tk, ahead=ahead)

    def q_tiles(fix_rope):
        return [jnp.concatenate([q_ref[i * tq:(i + 1) * tq, :LANE],
                                 fix_rope(q_ref[i * tq:(i + 1) * tq, LANE:2 * LANE])], axis=1)
                for i in range(nq)]

    @pl.when(worst <= FAST_SOFTMAX_MAX_BOUND)
    def _():
        tiles = q_tiles(lambda r: r * lane_scale)
        _attn_stream(tiles, k_ref, vt_ref, o_ref, online_max=False, **kw)

    @pl.when(jnp.logical_not(worst <= FAST_SOFTMAX_MAX_BOUND))
    def _():
        tiles = q_tiles(lambda r: jnp.where(bound_lane, jnp.zeros((), BF16), r))
        _attn_stream(tiles, k_ref, vt_ref, o_ref, online_max=True, **kw)


def attention(q, k, vt, *, batch, seq, heads, tq=512, nq=8, tk=1024, ahead=2):
    dq = q.shape[-1] // heads
    dv = vt.shape[1] // heads
    tqb = tq * nq
    return pl.pallas_call(
        functools.partial(_attn_kernel, tq=tq, tk=tk, ahead=ahead),
        grid=(batch, heads, seq // tqb),
        in_specs=[pl.BlockSpec((None, tqb, dq), lambda b, h, i: (b, i, h)),
                  pl.BlockSpec((None, seq, dq), lambda b, h, i: (b, 0, h)),
                  pl.BlockSpec((None, dv, seq), lambda b, h, i: (b, h, 0))],
        out_specs=pl.BlockSpec((None, tqb, dv), lambda b, h, i: (b, i, h)),
        out_shape=jax.ShapeDtypeStruct((batch, seq, heads * dv), BF16),
        compiler_params=_params(dimension_semantics=("arbitrary",) * 3),
        name="attention",
    )(q, k, vt)


def _rot_half_cols(w):
    half = w.shape[-1] // 2
    return jnp.concatenate([-w[..., half:], w[..., :half]], axis=-1)


def kernel(x, p, positions, pre_mix_norm, post_mix_norm, pre_mlp_norm, post_mlp_norm,
           w_mlp_in, w_mlp_out, w_ple_proj, w_ple_gate, ple_norm,
           hg_lb_logits, hg_w_in, hg_o_norm, hg_w_out,
           mla_w_in, mla_q_norm, mla_w_uq, mla_kv_norm, mla_w_ukv, mla_w_o):
    batch, seq, d = x.shape
    depth = p.shape[0]
    t = batch * seq
    hg_heads, hg_vdim = hg_o_norm.shape[1], hg_o_norm.shape[2]
    hg_kdim = hg_lb_logits.shape[1] // hg_heads
    q_rank = mla_q_norm.shape[1]
    kv_rank = mla_kv_norm.shape[1]
    rope = mla_w_in.shape[2] - q_rank - kv_rank
    mla_heads = mla_w_o.shape[1] // LANE
    nope = mla_w_uq.shape[2] // mla_heads - rope
    vdim = mla_w_ukv.shape[2] // mla_heads - nope
    assert nope == LANE and vdim == LANE and 2 * rope == LANE

    row = lambda g: g.reshape(1, -1).astype(F32)
    h = x.reshape(t, d)
    pos = positions.reshape(t, 1)
    inv = 1.0 / (ROPE_BASE ** (jnp.arange(0, rope, 2, dtype=F32) / rope))
    inv = jnp.tile(inv, LANE // inv.shape[0]).reshape(1, LANE)
    qk_scale = (nope + rope) ** -0.5 * math.log2(math.e)

    for i in range(depth):
        j = i // 2
        if i % 2 == 0:
            proj = norm_matmul(h, row(pre_mix_norm[i]), hg_w_in[j].astype(BF16))
            o_f, o_b = gla_bidirectional(
                proj.reshape(batch, seq, -1), hg_lb_logits.astype(F32), layer=i,
                batch=batch, seq=seq, heads=hg_heads, kdim=hg_kdim)
            hv = hg_heads * hg_vdim
            tile = lambda col: pl.BlockSpec((TAIL_ROWS, hv), lambda r: (r, col))
            mixer_in = (o_f.reshape(t, -1), o_b.reshape(t, -1), proj, row(hg_o_norm[j]))
            mixer_specs = (tile(0), tile(0), tile(4), _const_spec((1, hv)))
            w_mix, tail_kw = hg_w_out[j], dict(hgrn_heads=hg_heads, hgrn_vdim=hg_vdim)
        else:
            w_in = mla_w_in[j]
            c0 = q_rank + kv_rank
            w_kr = w_in[:, c0:]
            w_krr = _rot_half_cols(w_kr)
            w_in_ext = jnp.concatenate([w_in[:, :c0], w_kr, w_kr, w_krr, w_krr],
                                       axis=-1).astype(BF16)
            w_uq = mla_w_uq[j].reshape(q_rank, mla_heads, nope + rope)
            wn = w_uq[..., :nope].reshape(q_rank, -1).astype(BF16)
            w_qr = w_uq[..., nope:]
            wr = w_qr.reshape(q_rank, -1).astype(BF16)
            wrr = _rot_half_cols(w_qr).reshape(q_rank, -1).astype(BF16)
            w_ukv = mla_w_ukv[j].reshape(kv_rank, mla_heads, nope + vdim)
            wuk = w_ukv[..., :nope].reshape(kv_rank, -1).astype(BF16)
            wuv_t = w_ukv[..., nope:].reshape(kv_rank, -1).T.astype(BF16)
            q, k, vt = mla_in(h, pos, inv, row(pre_mix_norm[i]), w_in_ext,
                              row(mla_q_norm[j]), row(mla_kv_norm[j]), wn, wr, wrr, wuk, wuv_t,
                              batch=batch, heads=mla_heads, q_rank=q_rank, kv_rank=kv_rank,
                              qk_scale=qk_scale)
            o = attention(q.reshape(batch, seq, -1), k.reshape(batch, seq, -1), vt,
                          batch=batch, seq=seq, heads=mla_heads)
            o = o.reshape(t, -1)
            mixer_in = (o,)
            mixer_specs = (pl.BlockSpec((TAIL_ROWS, o.shape[1]), lambda r: (r, 0)),)
            w_mix, tail_kw = mla_w_o[j], {}
        consts = (w_mix.astype(BF16), row(post_mix_norm[i]), row(pre_mlp_norm[i]),
                  w_mlp_in[i].astype(BF16), w_mlp_out[i].astype(BF16),
                  row(post_mlp_norm[i]), w_ple_proj[i].astype(BF16),
                  w_ple_gate[i].astype(BF16), row(ple_norm[i]))
        h = layer_tail(mixer_in, mixer_specs, h, p[i].reshape(t, -1), consts,
                       tm=TAIL_ROWS, **tail_kw)
    return h.reshape(batch, seq, d)
```

```python
import functools
import math

import jax
import jax.numpy as jnp
from jax import lax
from jax.experimental import pallas as pl
from jax.experimental.pallas import tpu as pltpu

EPS = 1e-6
ROPE_BASE = 10000.0
LANE = 128
GLA_CHUNK = 64
VMEM_LIMIT = 56 * 1024 * 1024
TAIL_ROWS = 512
Q_BOUND_LANE = 64
K_NORM_LANE = 65
BOUND_MARGIN = 1.01
FAST_SOFTMAX_MAX_BOUND = 48.0

F32 = jnp.float32
BF16 = jnp.bfloat16


def _rms(x, gain):
    ms = jnp.mean(x * x, axis=-1, keepdims=True)
    return x * lax.rsqrt(ms + EPS) * gain


def _params(**kw):
    return pltpu.CompilerParams(vmem_limit_bytes=VMEM_LIMIT, **kw)


def _const_spec(shape, layer=None):
    if layer is None:
        nd = len(shape)
        return pl.BlockSpec(shape, lambda *_: (0,) * nd, pipeline_mode=pl.Buffered(1))
    nd = len(shape) - 1
    return pl.BlockSpec((None,) + tuple(shape[1:]), lambda *_: (layer,) + (0,) * nd,
                        pipeline_mode=pl.Buffered(1))


def _norm_matmul_kernel(x_ref, g_ref, w_ref, o_ref, *, tn):
    xn = _rms(x_ref[...], g_ref[...]).astype(BF16)
    for c in range(w_ref.shape[1] // tn):
        cols = slice(c * tn, (c + 1) * tn)
        o_ref[:, cols] = jnp.dot(xn, w_ref[:, cols],
                                 preferred_element_type=F32).astype(o_ref.dtype)


def norm_matmul(x, gain, w, *, tm=512, tn=1024):
    t, d = x.shape
    n = w.shape[1]
    assert t % tm == 0 and n % tn == 0
    return pl.pallas_call(
        functools.partial(_norm_matmul_kernel, tn=tn),
        grid=(t // tm,),
        in_specs=[pl.BlockSpec((tm, d), lambda i: (i, 0)),
                  _const_spec((1, d)), _const_spec((d, n))],
        out_specs=pl.BlockSpec((tm, n), lambda i: (i, 0)),
        out_shape=jax.ShapeDtypeStruct((t, n), BF16),
        compiler_params=_params(dimension_semantics=("arbitrary",)),
        name="norm_matmul",
    )(x, gain, w)


def _gla_kernel(lbl_ref, qf_ref, zf_ref, vf_ref, qb_ref, zb_ref, vb_ref,
                of_ref, ob_ref, st_ref, *, layer, heads, kdim, nsub):
    c = GLA_CHUNK

    @pl.when(pl.program_id(1) == 0)
    def _():
        st_ref[...] = jnp.zeros_like(st_ref)

    lg = lbl_ref[...].astype(F32)
    e = jnp.exp(lg - jnp.max(lg, axis=0, keepdims=True))
    lb = (jnp.sum(e[:layer + 1], axis=0, keepdims=True)
          / jnp.sum(e, axis=0, keepdims=True))

    row = lax.broadcasted_iota(jnp.int32, (c, c), 0)
    col = lax.broadcasted_iota(jnp.int32, (c, c), 1)
    keeps = (col <= row, col >= row)
    scale = kdim ** -0.5
    nt_dims = (((1,), (1,)), ((), ()))
    refs = ((qf_ref, zf_ref, vf_ref, of_ref), (qb_ref, zb_ref, vb_ref, ob_ref))

    def rows_of(direction, s):
        return pl.ds((s if direction == 0 else nsub - 1 - s) * c, c)

    def prep(direction, s):
        q_ref, z_ref, _, _ = refs[direction]
        rows = rows_of(direction, s)
        f = lb + (1.0 - lb) * jax.nn.sigmoid(z_ref[rows, :].astype(F32))
        g = jnp.log2(f)
        g_hi = g.astype(BF16)
        g_lo = (g - g_hi.astype(F32)).astype(BF16)
        tri = keeps[direction].astype(BF16)
        gc = (jnp.dot(tri, g_hi, preferred_element_type=F32)
              + jnp.dot(tri, g_lo, preferred_element_type=F32))
        decay = jnp.exp2(gc[c - 1:c, :] if direction == 0 else gc[0:1, :])
        q_dec = (q_ref[rows, :].astype(F32) * scale * jnp.exp2(gc)).astype(BF16)
        k_inv = (1.0 - f) * jnp.exp2(-gc)
        k_end = (k_inv * decay).astype(BF16)
        return q_dec, k_inv.astype(BF16), k_end, decay

    def state_stage(direction, s, prepped):
        q_dec, k_inv, k_end, decay = prepped
        v = refs[direction][2][rows_of(direction, s), :]
        out = []
        for h in range(heads):
            sl = slice(h * kdim, (h + 1) * kdim)
            a = lax.dot_general(q_dec[:, sl], k_inv[:, sl], nt_dims,
                                preferred_element_type=F32)
            st = st_ref[direction, h]
            o_state = lax.dot_general(q_dec[:, sl], st.astype(BF16), nt_dims,
                                      preferred_element_type=F32)
            st_ref[direction, h] = st * decay[:, sl] + lax.dot_general(
                v[:, sl], k_end[:, sl], (((0,), (0,)), ((), ())),
                preferred_element_type=F32)
            out.append((a, o_state))
        return out

    def output_stage(direction, s, staged):
        _, _, v_ref, o_ref = refs[direction]
        rows = rows_of(direction, s)
        v = v_ref[rows, :]
        for h in range(heads):
            sl = slice(h * kdim, (h + 1) * kdim)
            a, o_state = staged[h]
            a = jnp.where(keeps[direction], a, 0.0).astype(BF16)
            o = jnp.dot(a, v[:, sl], preferred_element_type=F32) + o_state
            o_ref[rows, sl] = o.astype(o_ref.dtype)

    dirs = (0, 1)
    prepped = [prep(d, 0) for d in dirs]
    staged_prev = None
    for s in range(nsub):
        prepped_next = [prep(d, s + 1) for d in dirs] if s + 1 < nsub else None
        staged = [state_stage(d, s, prepped[d]) for d in dirs]
        if staged_prev is not None:
            for d in dirs:
                output_stage(d, s - 1, staged_prev[d])
        staged_prev, prepped = staged, prepped_next
    for d in dirs:
        output_stage(d, nsub - 1, staged_prev[d])


def gla_bidirectional(proj, lb_logits, *, layer, batch, seq, heads, kdim, nsub=8):
    hk = heads * kdim
    tt = nsub * GLA_CHUNK
    assert seq % tt == 0
    nt = seq // tt
    nl = lb_logits.shape[0]

    def fwd(col):
        return pl.BlockSpec((None, tt, hk), lambda b, t: (b, t, col))

    def bwd(col):
        return pl.BlockSpec((None, tt, hk), lambda b, t: (b, nt - 1 - t, col))

    kern = functools.partial(_gla_kernel, layer=layer, heads=heads, kdim=kdim, nsub=nsub)
    return pl.pallas_call(
        kern,
        grid=(batch, nt),
        in_specs=[pl.BlockSpec((nl, hk), lambda b, t: (0, 0)),
                  fwd(0), fwd(1), fwd(3), bwd(0), bwd(2), bwd(3)],
        out_specs=[pl.BlockSpec((None, tt, hk), lambda b, t: (b, t, 0)),
                   pl.BlockSpec((None, tt, hk), lambda b, t: (b, nt - 1 - t, 0))],
        out_shape=[jax.ShapeDtypeStruct((batch, seq, hk), BF16)] * 2,
        scratch_shapes=[pltpu.VMEM((2, heads, kdim, kdim), F32)],
        compiler_params=_params(dimension_semantics=("arbitrary", "arbitrary")),
        name="gla_bidirectional",
    )(lb_logits, proj, proj, proj, proj, proj, proj)


def _hgrn_gated(of_ref, ob_ref, gate_ref, on_ref, rows, *, heads, vdim):
    o = of_ref[rows, :].astype(F32) + ob_ref[rows, :].astype(F32)
    parts = []
    for h in range(heads):
        oh = o[:, h * vdim:(h + 1) * vdim]
        ms = jnp.mean(oh * oh, axis=-1, keepdims=True)
        parts.append(oh * lax.rsqrt(ms + EPS))
    on = jnp.concatenate(parts, axis=-1) * on_ref[...]
    gate = gate_ref[rows, :].astype(F32)
    return (on * (gate * jax.nn.sigmoid(gate))).astype(BF16)


def _layer_tail_kernel(*refs, hgrn_heads, hgrn_vdim, tf, splits):
    if hgrn_heads:
        of_ref, ob_ref, gate_ref, on_ref = refs[:4]
        mixer_out = lambda rows: _hgrn_gated(of_ref, ob_ref, gate_ref, on_ref, rows,
                                             heads=hgrn_heads, vdim=hgrn_vdim)
        refs = refs[4:]
    else:
        mixer_out = lambda rows: refs[0][rows, :]
    (h_ref, p_ref, wm_ref, gm_ref, g1_ref, w1_ref, w2_ref, g2_ref, wp_ref, wg_ref, g3_ref,
     o_ref) = refs[-12:]
    dff = w1_ref.shape[1]
    rs = h_ref.shape[0] // splits

    def mm(x, w):
        return jnp.dot(x, w, preferred_element_type=F32)

    def group(g):
        rows = slice(g * rs, (g + 1) * rs)
        y = mixer_out(rows)
        h = h_ref[rows, :] + _rms(mm(y, wm_ref[...]), gm_ref[...])
        a = _rms(h, g1_ref[...]).astype(BF16)
        yield
        acc = None
        for c in range(dff // tf):
            u = jnp.maximum(mm(a, w1_ref[:, c * tf:(c + 1) * tf]), 0.0)
            part = mm((u * u).astype(BF16), w2_ref[c * tf:(c + 1) * tf, :])
            acc = part if acc is None else acc + part
        yield
        h2 = h + _rms(acc, g2_ref[...])
        e = mm(p_ref[rows, :].astype(BF16), wp_ref[...])
        gt = jax.nn.sigmoid(mm(h2.astype(BF16), wg_ref[...]))
        o_ref[rows, :] = h2 + _rms(gt * e, g3_ref[...])

    live = [group(g) for g in range(splits)]
    while live:
        for gen in list(live):
            if next(gen, StopIteration) is StopIteration:
                live.remove(gen)


def layer_tail(mixer_in, mixer_specs, h, p, consts, *, layer, hgrn_heads=0, hgrn_vdim=0,
               tm=512, tf=1024, splits=2):
    t, d = h.shape
    pd = p.shape[2]
    assert t % tm == 0 and tm % (8 * splits) == 0 and consts[3].shape[2] % tf == 0
    row = lambda i: (i, 0)
    kern = functools.partial(_layer_tail_kernel, hgrn_heads=hgrn_heads, hgrn_vdim=hgrn_vdim,
                             tf=tf, splits=splits)
    return pl.pallas_call(
        kern,
        grid=(t // tm,),
        in_specs=list(mixer_specs)
                 + [pl.BlockSpec((tm, d), row),
                    pl.BlockSpec((None, tm, pd), lambda i: (layer, i, 0))]
                 + [_const_spec(c.shape, layer if c.ndim == 3 else None) for c in consts],
        out_specs=pl.BlockSpec((tm, d), row),
        out_shape=jax.ShapeDtypeStruct((t, d), F32),
        compiler_params=_params(dimension_semantics=("arbitrary",)),
        name="layer_tail",
    )(*mixer_in, h, p, *consts)


def _mla_in_kernel(h_ref, pos_ref, inv_ref, g_ref, win_ref, qg_ref, kvg_ref,
                   wn_ref, wr_ref, wrr_ref, wuk_ref, wuv_ref,
                   en_ref, er_ref, ones_ref,
                   q_ref, k_ref, v_ref, *, heads, q_rank, kv_rank, qk_scale):
    a = _rms(h_ref[...], g_ref[...]).astype(BF16)
    proj = jnp.dot(a, win_ref[...], preferred_element_type=F32)
    cq = _rms(proj[:, :q_rank], qg_ref[...]).astype(BF16)
    c0 = q_rank + kv_rank
    ckv = _rms(proj[:, q_rank:c0], kvg_ref[...]).astype(BF16)
    ang = pos_ref[...].astype(F32) * inv_ref[...]
    cos = jnp.cos(ang)
    sin = jnp.sin(ang)
    lane = lax.broadcasted_iota(jnp.int32, (1, LANE), 1)
    low_half = lane < LANE // 2
    kr = jnp.where(low_half,
                   proj[:, c0:c0 + LANE] * cos + proj[:, c0 + LANE:c0 + 2 * LANE] * sin, 0.0)
    kr_ones = jnp.where(lane == Q_BOUND_LANE, 1.0, kr)
    qn = jnp.dot(cq, wn_ref[...], preferred_element_type=F32) * qk_scale
    cos_t = jnp.tile(cos * qk_scale, (1, heads // 2))
    sin_t = jnp.tile(sin * qk_scale, (1, heads // 2))
    qr = (jnp.dot(cq, wr_ref[...], preferred_element_type=F32) * cos_t
          + jnp.dot(cq, wrr_ref[...], preferred_element_type=F32) * sin_t)
    kn = jnp.dot(ckv, wuk_ref[...], preferred_element_type=F32)

    def sq(x):
        return (x * x).astype(BF16)

    def mm(x, w_ref):
        return jnp.dot(x, w_ref[...], preferred_element_type=F32)

    q_bound = -BOUND_MARGIN * jnp.sqrt(mm(sq(qn), en_ref) + mm(sq(qr), er_ref))
    k_norm = BOUND_MARGIN * jnp.sqrt(mm(sq(kn), en_ref) + mm(sq(kr), ones_ref))
    for h in range(heads):
        sl = slice(h * LANE, (h + 1) * LANE)
        pair = qr[:, (h // 2) * LANE:(h // 2 + 1) * LANE]
        if h % 2:
            pair = pltpu.roll(pair, LANE // 2, axis=1)
        q_lane = pltpu.roll(q_bound, (Q_BOUND_LANE - h) % LANE, axis=1)
        k_lane = pltpu.roll(k_norm, (K_NORM_LANE - h) % LANE, axis=1)
        q_ref[:, 2 * h * LANE:(2 * h + 1) * LANE] = qn[:, sl].astype(BF16)
        q_ref[:, (2 * h + 1) * LANE:(2 * h + 2) * LANE] = jnp.where(
            lane == Q_BOUND_LANE, q_lane, jnp.where(low_half, pair, 0.0)).astype(BF16)
        k_ref[:, 2 * h * LANE:(2 * h + 1) * LANE] = kn[:, sl].astype(BF16)
        k_ref[:, (2 * h + 1) * LANE:(2 * h + 2) * LANE] = jnp.where(
            lane == K_NORM_LANE, k_lane, kr_ones).astype(BF16)
    v_ref[...] = lax.dot_general(wuv_ref[...], ckv, (((1,), (1,)), ((), ())),
                                 preferred_element_type=F32).astype(BF16)


def mla_in(h, pos, inv, gain, w_in, q_gain, kv_gain, wn, wr, wrr, wuk, wuv_t,
           *, batch, heads, q_rank, kv_rank, qk_scale, tm=256):
    t, d = h.shape
    assert (t // batch) % tm == 0
    nsb = t // batch // tm
    row = lambda i: (i, 0)
    kern = functools.partial(_mla_in_kernel, heads=heads, q_rank=q_rank, kv_rank=kv_rank,
                             qk_scale=qk_scale)
    col = jnp.arange(LANE)[None, :]
    e_nope = (jnp.arange(heads * LANE)[:, None] // LANE == col).astype(BF16)
    e_rope = (jnp.arange(heads * LANE // 2)[:, None] // (LANE // 2) == col).astype(BF16)
    ones = jnp.ones((LANE, LANE), BF16)
    consts = [inv, gain, w_in, q_gain, kv_gain, wn, wr, wrr, wuk, wuv_t, e_nope, e_rope, ones]
    return pl.pallas_call(
        kern,
        grid=(t // tm,),
        in_specs=[pl.BlockSpec((tm, d), row), pl.BlockSpec((tm, 1), row)]
                 + [_const_spec(c.shape) for c in consts],
        out_specs=[pl.BlockSpec((tm, 2 * heads * LANE), row),
                   pl.BlockSpec((tm, 2 * heads * LANE), row),
                   pl.BlockSpec((None, heads * LANE, tm), lambda i: (i // nsb, 0, i % nsb))],
        out_shape=[jax.ShapeDtypeStruct((t, 2 * heads * LANE), BF16),
                   jax.ShapeDtypeStruct((t, 2 * heads * LANE), BF16),
                   jax.ShapeDtypeStruct((batch, heads * LANE, t // batch), BF16)],
        compiler_params=_params(dimension_semantics=("arbitrary",)),
        name="mla_in",
    )(h, pos, *consts)


def _attn_stream(q_tiles, k_ref, vt_ref, o_ref, *, tq, tk, ahead, online_max):
    nk = k_ref.shape[0] // tk
    total = len(q_tiles) * nk

    def scores(t):
        i, j = divmod(t, nk)
        return lax.dot_general(k_ref[j * tk:(j + 1) * tk, :], q_tiles[i],
                               (((1,), (1,)), ((), ())),
                               preferred_element_type=F32)

    pending = [scores(t) for t in range(min(ahead, total))]
    for t in range(total):
        i, j = divmod(t, nk)
        s = pending.pop(0)
        if t + ahead < total:
            pending.append(scores(t + ahead))
        if online_max:
            s_max = jnp.max(s, axis=0, keepdims=True)
            m_new = s_max if j == 0 else jnp.maximum(m, s_max)
            s = s - m_new
        p = jnp.exp2(s)
        p_sum = jnp.sum(p, axis=0, keepdims=True)
        pv = jnp.dot(vt_ref[:, j * tk:(j + 1) * tk], p.astype(BF16),
                     preferred_element_type=F32)
        if j == 0:
            l, acc = p_sum, pv
        elif online_max:
            alpha = jnp.exp2(m - m_new)
            l = alpha * l + p_sum
            acc = alpha * acc + pv
        else:
            l = l + p_sum
            acc = acc + pv
        if online_max:
            m = m_new
        if j == nk - 1:
            o_ref[i * tq:(i + 1) * tq, :] = (acc / l).T.astype(o_ref.dtype)


def _attn_kernel(q_ref, k_ref, vt_ref, o_ref, *, tq, tk, ahead):
    nq = q_ref.shape[0] // tq
    lane = lax.broadcasted_iota(jnp.int32, (1, LANE), 1)
    bound_lane = lane == Q_BOUND_LANE

    def over_rows(x, op):
        while x.shape[0] > 16:
            half = x.shape[0] // 2
            x = op(x[:half], x[half:])
        return x.astype(F32)

    k_top = jnp.max(over_rows(k_ref[:, LANE:2 * LANE], jnp.maximum), axis=0, keepdims=True)
    k_max = jnp.max(jnp.where(lane == K_NORM_LANE, k_top, 0.0), axis=1, keepdims=True)
    q_low = jnp.min(over_rows(q_ref[:, LANE:2 * LANE], jnp.minimum), axis=0, keepdims=True)
    q_max = jnp.max(jnp.where(bound_lane, -q_low, 0.0), axis=1, keepdims=True)
    worst = jnp.max(q_max * k_max)
    lane_scale = jnp.where(bound_lane, k_max, 1.0).astype(BF16)
    kw = dict(tq=tq, tk=tk, ahead=ahead)

    def q_tiles(fix_rope):
        return [jnp.concatenate([q_ref[i * tq:(i + 1) * tq, :LANE],
                                 fix_rope(q_ref[i * tq:(i + 1) * tq, LANE:2 * LANE])], axis=1)
                for i in range(nq)]

    @pl.when(worst <= FAST_SOFTMAX_MAX_BOUND)
    def _():
        tiles = q_tiles(lambda r: r * lane_scale)
        _attn_stream(tiles, k_ref, vt_ref, o_ref, online_max=False, **kw)

    @pl.when(jnp.logical_not(worst <= FAST_SOFTMAX_MAX_BOUND))
    def _():
        tiles = q_tiles(lambda r: jnp.where(bound_lane, jnp.zeros((), BF16), r))
        _attn_stream(tiles, k_ref, vt_ref, o_ref, online_max=True, **kw)


def attention(q, k, vt, *, batch, seq, heads, tq=512, nq=8, tk=1024, ahead=2):
    dq = q.shape[-1] // heads
    dv = vt.shape[1] // heads
    tqb = tq * nq
    assert seq % tqb == 0 and seq % tk == 0
    return pl.pallas_call(
        functools.partial(_attn_kernel, tq=tq, tk=tk, ahead=ahead),
        grid=(batch, heads, seq // tqb),
        in_specs=[pl.BlockSpec((None, tqb, dq), lambda b, h, i: (b, i, h)),
                  pl.BlockSpec((None, seq, dq), lambda b, h, i: (b, 0, h)),
                  pl.BlockSpec((None, dv, seq), lambda b, h, i: (b, h, 0))],
        out_specs=pl.BlockSpec((None, tqb, dv), lambda b, h, i: (b, i, h)),
        out_shape=jax.ShapeDtypeStruct((batch, seq, heads * dv), BF16),
        compiler_params=_params(dimension_semantics=("arbitrary",) * 3),
        name="attention",
    )(q, k, vt)


def _rot_half_cols(w):
    half = w.shape[-1] // 2
    return jnp.concatenate([-w[..., half:], w[..., :half]], axis=-1)


def kernel(x, p, positions, pre_mix_norm, post_mix_norm, pre_mlp_norm, post_mlp_norm,
           w_mlp_in, w_mlp_out, w_ple_proj, w_ple_gate, ple_norm,
           hg_lb_logits, hg_w_in, hg_o_norm, hg_w_out,
           mla_w_in, mla_q_norm, mla_w_uq, mla_kv_norm, mla_w_ukv, mla_w_o):
    batch, seq, d = x.shape
    depth = p.shape[0]
    t = batch * seq
    hg_heads, hg_vdim = hg_o_norm.shape[1], hg_o_norm.shape[2]
    hg_kdim = hg_lb_logits.shape[1] // hg_heads
    q_rank = mla_q_norm.shape[1]
    kv_rank = mla_kv_norm.shape[1]
    rope = mla_w_in.shape[2] - q_rank - kv_rank
    mla_heads = mla_w_o.shape[1] // LANE
    nope = mla_w_uq.shape[2] // mla_heads - rope
    vdim = mla_w_ukv.shape[2] // mla_heads - nope
    assert nope == LANE and vdim == LANE and 2 * rope == LANE

    row = lambda g: g.reshape(1, -1).astype(F32)
    h = x.reshape(t, d)
    pos = positions.reshape(t, 1)
    inv = 1.0 / (ROPE_BASE ** (jnp.arange(0, rope, 2, dtype=F32) / rope))
    inv = jnp.tile(inv, LANE // inv.shape[0]).reshape(1, LANE)
    qk_scale = (nope + rope) ** -0.5 * math.log2(math.e)

    for i in range(depth):
        j = i // 2
        if i % 2 == 0:
            proj = norm_matmul(h, row(pre_mix_norm[i]), hg_w_in[j].astype(BF16))
            o_f, o_b = gla_bidirectional(
                proj.reshape(batch, seq, -1), hg_lb_logits.astype(F32), layer=i,
                batch=batch, seq=seq, heads=hg_heads, kdim=hg_kdim)
            hv = hg_heads * hg_vdim
            tile = lambda col: pl.BlockSpec((TAIL_ROWS, hv), lambda r: (r, col))
            mixer_in = (o_f.reshape(t, -1), o_b.reshape(t, -1), proj, row(hg_o_norm[j]))
            mixer_specs = (tile(0), tile(0), tile(4), _const_spec((1, hv)))
            w_mix, tail_kw = hg_w_out[j], dict(hgrn_heads=hg_heads, hgrn_vdim=hg_vdim)
        else:
            w_in = mla_w_in[j]
            c0 = q_rank + kv_rank
            w_kr = w_in[:, c0:]
            w_krr = _rot_half_cols(w_kr)
            w_in_ext = jnp.concatenate([w_in[:, :c0], w_kr, w_kr, w_krr, w_krr],
                                       axis=-1).astype(BF16)
            w_uq = mla_w_uq[j].reshape(q_rank, mla_heads, nope + rope)
            wn = w_uq[..., :nope].reshape(q_rank, -1).astype(BF16)
            w_qr = w_uq[..., nope:]
            wr = w_qr.reshape(q_rank, -1).astype(BF16)
            wrr = _rot_half_cols(w_qr).reshape(q_rank, -1).astype(BF16)
            w_ukv = mla_w_ukv[j].reshape(kv_rank, mla_heads, nope + vdim)
            wuk = w_ukv[..., :nope].reshape(kv_rank, -1).astype(BF16)
            wuv_t = w_ukv[..., nope:].reshape(kv_rank, -1).T.astype(BF16)
            q, k, vt = mla_in(h, pos, inv, row(pre_mix_norm[i]), w_in_ext,
                              row(mla_q_norm[j]), row(mla_kv_norm[j]), wn, wr, wrr, wuk, wuv_t,
                              batch=batch, heads=mla_heads, q_rank=q_rank, kv_rank=kv_rank,
                              qk_scale=qk_scale)
            o = attention(q.reshape(batch, seq, -1), k.reshape(batch, seq, -1), vt,
                          batch=batch, seq=seq, heads=mla_heads)
            o = o.reshape(t, -1)
            mixer_in = (o,)
            mixer_specs = (pl.BlockSpec((TAIL_ROWS, o.shape[1]), lambda r: (r, 0)),)
            w_mix, tail_kw = mla_w_o[j], {}
        consts = (w_mix.astype(BF16), row(post_mix_norm[i]), row(pre_mlp_norm[i]),
                  w_mlp_in.astype(BF16), w_mlp_out.astype(BF16),
                  row(post_mlp_norm[i]), w_ple_proj.astype(BF16),
                  w_ple_gate.astype(BF16), row(ple_norm[i]))
        h = layer_tail(mixer_in, mixer_specs, h, p.reshape(depth, t, -1), consts,
                       layer=i, tm=TAIL_ROWS, **tail_kw)
    return h.reshape(batch, seq, d)
```

```python
import functools
import math

import jax
import jax.numpy as jnp
from jax import lax
from jax.experimental import pallas as pl
from jax.experimental.pallas import tpu as pltpu

EPS = 1e-6
ROPE_BASE = 10000.0
LANE = 128
GLA_CHUNK = 64
VMEM_LIMIT = 56 * 1024 * 1024
TAIL_ROWS = 512
Q_BOUND_LANE = 64
K_NORM_LANE = 65
BOUND_MARGIN = 1.01
FAST_SOFTMAX_MAX_BOUND = 48.0

F32 = jnp.float32
BF16 = jnp.bfloat16


def _rms(x, gain):
    ms = jnp.mean(x * x, axis=-1, keepdims=True)
    return x * lax.rsqrt(ms + EPS) * gain


def _params(**kw):
    return pltpu.CompilerParams(vmem_limit_bytes=VMEM_LIMIT, **kw)


def _const_spec(shape, layer=None):
    if layer is None:
        nd = len(shape)
        return pl.BlockSpec(shape, lambda *_: (0,) * nd, pipeline_mode=pl.Buffered(1))
    nd = len(shape) - 1
    return pl.BlockSpec((None,) + tuple(shape[1:]), lambda *_: (layer,) + (0,) * nd,
                        pipeline_mode=pl.Buffered(1))


def _norm_matmul_kernel(x_ref, g_ref, w_ref, o_ref, *, tn):
    xn = _rms(x_ref[...], g_ref[...]).astype(BF16)
    for c in range(w_ref.shape[1] // tn):
        cols = slice(c * tn, (c + 1) * tn)
        o_ref[:, cols] = jnp.dot(xn, w_ref[:, cols],
                                 preferred_element_type=F32).astype(o_ref.dtype)


def norm_matmul(x, gain, w, *, tm=512, tn=1024):
    t, d = x.shape
    n = w.shape[1]
    assert t % tm == 0 and n % tn == 0
    return pl.pallas_call(
        functools.partial(_norm_matmul_kernel, tn=tn),
        grid=(t // tm,),
        in_specs=[pl.BlockSpec((tm, d), lambda i: (i, 0)),
                  _const_spec((1, d)), _const_spec((d, n))],
        out_specs=pl.BlockSpec((tm, n), lambda i: (i, 0)),
        out_shape=jax.ShapeDtypeStruct((t, n), BF16),
        compiler_params=_params(dimension_semantics=("arbitrary",)),
        name="norm_matmul",
    )(x, gain, w)


def _gla_kernel(lbl_ref, qf_ref, zf_ref, vf_ref, qb_ref, zb_ref, vb_ref,
                of_ref, ob_ref, st_ref, *, layer, heads, kdim, nsub):
    c = GLA_CHUNK

    @pl.when(pl.program_id(1) == 0)
    def _():
        st_ref[...] = jnp.zeros_like(st_ref)

    lg = lbl_ref[...].astype(F32)
    e = jnp.exp(lg - jnp.max(lg, axis=0, keepdims=True))
    lb = (jnp.sum(e[:layer + 1], axis=0, keepdims=True)
          / jnp.sum(e, axis=0, keepdims=True))

    row = lax.broadcasted_iota(jnp.int32, (c, c), 0)
    col = lax.broadcasted_iota(jnp.int32, (c, c), 1)
    keeps = (col <= row, col >= row)
    scale = kdim ** -0.5
    nt_dims = (((1,), (1,)), ((), ()))
    refs = ((qf_ref, zf_ref, vf_ref, of_ref), (qb_ref, zb_ref, vb_ref, ob_ref))

    def rows_of(direction, s):
        return pl.ds((s if direction == 0 else nsub - 1 - s) * c, c)

    def prep(direction, s):
        q_ref, z_ref, _, _ = refs[direction]
        rows = rows_of(direction, s)
        f = lb + (1.0 - lb) * jax.nn.sigmoid(z_ref[rows, :].astype(F32))
        g = jnp.log2(f)
        g_hi = g.astype(BF16)
        g_lo = (g - g_hi.astype(F32)).astype(BF16)
        tri = keeps[direction].astype(BF16)
        gc = (jnp.dot(tri, g_hi, preferred_element_type=F32)
              + jnp.dot(tri, g_lo, preferred_element_type=F32))
        decay = jnp.exp2(gc[c - 1:c, :] if direction == 0 else gc[0:1, :])
        q_dec = (q_ref[rows, :].astype(F32) * scale * jnp.exp2(gc)).astype(BF16)
        k_inv = (1.0 - f) * jnp.exp2(-gc)
        k_end = (k_inv * decay).astype(BF16)
        return q_dec, k_inv.astype(BF16), k_end, decay

    def state_stage(direction, s, prepped):
        q_dec, k_inv, k_end, decay = prepped
        v = refs[direction][2][rows_of(direction, s), :]
        out = []
        for h in range(heads):
            sl = slice(h * kdim, (h + 1) * kdim)
            a = lax.dot_general(q_dec[:, sl], k_inv[:, sl], nt_dims,
                                preferred_element_type=F32)
            st = st_ref[direction, h]
            o_state = lax.dot_general(q_dec[:, sl], st.astype(BF16), nt_dims,
                                      preferred_element_type=F32)
            st_ref[direction, h] = st * decay[:, sl] + lax.dot_general(
                v[:, sl], k_end[:, sl], (((0,), (0,)), ((), ())),
                preferred_element_type=F32)
            out.append((a, o_state))
        return out

    def output_stage(direction, s, staged):
        _, _, v_ref, o_ref = refs[direction]
        rows = rows_of(direction, s)
        v = v_ref[rows, :]
        for h in range(heads):
            sl = slice(h * kdim, (h + 1) * kdim)
            a, o_state = staged[h]
            a = jnp.where(keeps[direction], a, 0.0).astype(BF16)
            o = jnp.dot(a, v[:, sl], preferred_element_type=F32) + o_state
            o_ref[rows, sl] = o.astype(o_ref.dtype)

    dirs = (0, 1)
    prepped = [prep(d, 0) for d in dirs]
    staged_prev = None
    for s in range(nsub):
        prepped_next = [prep(d, s + 1) for d in dirs] if s + 1 < nsub else None
        staged = [state_stage(d, s, prepped[d]) for d in dirs]
        if staged_prev is not None:
            for d in dirs:
                output_stage(d, s - 1, staged_prev[d])
        staged_prev, prepped = staged, prepped_next
    for d in dirs:
        output_stage(d, nsub - 1, staged_prev[d])


def gla_bidirectional(proj, lb_logits, *, layer, batch, seq, heads, kdim, nsub=16):
    hk = heads * kdim
    tt = nsub * GLA_CHUNK
    assert seq % tt == 0
    nt = seq // tt
    nl = lb_logits.shape[0]

    def fwd(col):
        return pl.BlockSpec((None, tt, hk), lambda b, t: (b, t, col))

    def bwd(col):
        return pl.BlockSpec((None, tt, hk), lambda b, t: (b, nt - 1 - t, col))

    kern = functools.partial(_gla_kernel, layer=layer, heads=heads, kdim=kdim, nsub=nsub)
    return pl.pallas_call(
        kern,
        grid=(batch, nt),
        in_specs=[pl.BlockSpec((nl, hk), lambda b, t: (0, 0)),
                  fwd(0), fwd(1), fwd(3), bwd(0), bwd(2), bwd(3)],
        out_specs=[pl.BlockSpec((None, tt, hk), lambda b, t: (b, t, 0)),
                   pl.BlockSpec((None, tt, hk), lambda b, t: (b, nt - 1 - t, 0))],
        out_shape=[jax.ShapeDtypeStruct((batch, seq, hk), BF16)] * 2,
        scratch_shapes=[pltpu.VMEM((2, heads, kdim, kdim), F32)],
        compiler_params=_params(dimension_semantics=("arbitrary", "arbitrary")),
        name="gla_bidirectional",
    )(lb_logits, proj, proj, proj, proj, proj, proj)


def _hgrn_gated(of_ref, ob_ref, gate_ref, on_ref, rows, *, heads, vdim):
    o = of_ref[rows, :].astype(F32) + ob_ref[rows, :].astype(F32)
    parts = []
    for h in range(heads):
        oh = o[:, h * vdim:(h + 1) * vdim]
        ms = jnp.mean(oh * oh, axis=-1, keepdims=True)
        parts.append(oh * lax.rsqrt(ms + EPS))
    on = jnp.concatenate(parts, axis=-1) * on_ref[...]
    gate = gate_ref[rows, :].astype(F32)
    return (on * (gate * jax.nn.sigmoid(gate))).astype(BF16)


def _layer_tail_kernel(*refs, hgrn_heads, hgrn_vdim, tf, splits):
    if hgrn_heads:
        of_ref, ob_ref, gate_ref, on_ref = refs[:4]
        mixer_out = lambda rows: _hgrn_gated(of_ref, ob_ref, gate_ref, on_ref, rows,
                                             heads=hgrn_heads, vdim=hgrn_vdim)
        refs = refs[4:]
    else:
        mixer_out = lambda rows: refs[0][rows, :]
    (h_ref, p_ref, wm_ref, gm_ref, g1_ref, w1_ref, w2_ref, g2_ref, wp_ref, wg_ref, g3_ref,
     o_ref) = refs[-12:]
    dff = w1_ref.shape[1]
    rs = h_ref.shape[0] // splits

    def mm(x, w):
        return jnp.dot(x, w, preferred_element_type=F32)

    def group(g):
        rows = slice(g * rs, (g + 1) * rs)
        y = mixer_out(rows)
        h = h_ref[rows, :] + _rms(mm(y, wm_ref[...]), gm_ref[...])
        a = _rms(h, g1_ref[...]).astype(BF16)
        yield
        acc = None
        for c in range(dff // tf):
            u = jnp.maximum(mm(a, w1_ref[:, c * tf:(c + 1) * tf]), 0.0)
            part = mm((u * u).astype(BF16), w2_ref[c * tf:(c + 1) * tf, :])
            acc = part if acc is None else acc + part
        yield
        h2 = h + _rms(acc, g2_ref[...])
        e = mm(p_ref[rows, :].astype(BF16), wp_ref[...])
        gt = jax.nn.sigmoid(mm(h2.astype(BF16), wg_ref[...]))
        o_ref[rows, :] = h2 + _rms(gt * e, g3_ref[...])

    live = [group(g) for g in range(splits)]
    while live:
        for gen in list(live):
            if next(gen, StopIteration) is StopIteration:
                live.remove(gen)


def layer_tail(mixer_in, mixer_specs, h, p, consts, *, layer, hgrn_heads=0, hgrn_vdim=0,
               tm=512, tf=1024, splits=2):
    t, d = h.shape
    pd = p.shape[2]
    assert t % tm == 0 and tm % (8 * splits) == 0 and consts[3].shape[2] % tf == 0
    row = lambda i: (i, 0)
    kern = functools.partial(_layer_tail_kernel, hgrn_heads=hgrn_heads, hgrn_vdim=hgrn_vdim,
                             tf=tf, splits=splits)
    return pl.pallas_call(
        kern,
        grid=(t // tm,),
        in_specs=list(mixer_specs)
                 + [pl.BlockSpec((tm, d), row),
                    pl.BlockSpec((None, tm, pd), lambda i: (layer, i, 0))]
                 + [_const_spec(c.shape, layer if c.ndim == 3 else None) for c in consts],
        out_specs=pl.BlockSpec((tm, d), row),
        out_shape=jax.ShapeDtypeStruct((t, d), F32),
        compiler_params=_params(dimension_semantics=("arbitrary",)),
        name="layer_tail",
    )(*mixer_in, h, p, *consts)


def _mla_in_kernel(h_ref, pos_ref, inv_ref, g_ref, win_ref, qg_ref, kvg_ref,
                   wn_ref, wr_ref, wrr_ref, wuk_ref, wuv_ref,
                   en_ref, er_ref, ones_ref,
                   q_ref, k_ref, v_ref, *, heads, q_rank, kv_rank, qk_scale):
    a = _rms(h_ref[...], g_ref[...]).astype(BF16)
    proj = jnp.dot(a, win_ref[...], preferred_element_type=F32)
    cq = _rms(proj[:, :q_rank], qg_ref[...]).astype(BF16)
    c0 = q_rank + kv_rank
    ckv = _rms(proj[:, q_rank:c0], kvg_ref[...]).astype(BF16)
    ang = pos_ref[...].astype(F32) * inv_ref[...]
    cos = jnp.cos(ang)
    sin = jnp.sin(ang)
    lane = lax.broadcasted_iota(jnp.int32, (1, LANE), 1)
    low_half = lane < LANE // 2
    kr = jnp.where(low_half,
                   proj[:, c0:c0 + LANE] * cos + proj[:, c0 + LANE:c0 + 2 * LANE] * sin, 0.0)
    kr_ones = jnp.where(lane == Q_BOUND_LANE, 1.0, kr)
    qn = jnp.dot(cq, wn_ref[...], preferred_element_type=F32) * qk_scale
    cos_t = jnp.tile(cos * qk_scale, (1, heads // 2))
    sin_t = jnp.tile(sin * qk_scale, (1, heads // 2))
    qr = (jnp.dot(cq, wr_ref[...], preferred_element_type=F32) * cos_t
          + jnp.dot(cq, wrr_ref[...], preferred_element_type=F32) * sin_t)
    kn = jnp.dot(ckv, wuk_ref[...], preferred_element_type=F32)

    def sq(x):
        return (x * x).astype(BF16)

    def mm(x, w_ref):
        return jnp.dot(x, w_ref[...], preferred_element_type=F32)

    q_bound = -BOUND_MARGIN * jnp.sqrt(mm(sq(qn), en_ref) + mm(sq(qr), er_ref))
    k_norm = BOUND_MARGIN * jnp.sqrt(mm(sq(kn), en_ref) + mm(sq(kr), ones_ref))
    for h in range(heads):
        sl = slice(h * LANE, (h + 1) * LANE)
        pair = qr[:, (h // 2) * LANE:(h // 2 + 1) * LANE]
        if h % 2:
            pair = pltpu.roll(pair, LANE // 2, axis=1)
        q_lane = pltpu.roll(q_bound, (Q_BOUND_LANE - h) % LANE, axis=1)
        k_lane = pltpu.roll(k_norm, (K_NORM_LANE - h) % LANE, axis=1)
        q_ref[:, 2 * h * LANE:(2 * h + 1) * LANE] = qn[:, sl].astype(BF16)
        q_ref[:, (2 * h + 1) * LANE:(2 * h + 2) * LANE] = jnp.where(
            lane == Q_BOUND_LANE, q_lane, jnp.where(low_half, pair, 0.0)).astype(BF16)
        k_ref[:, 2 * h * LANE:(2 * h + 1) * LANE] = kn[:, sl].astype(BF16)
        k_ref[:, (2 * h + 1) * LANE:(2 * h + 2) * LANE] = jnp.where(
            lane == K_NORM_LANE, k_lane, kr_ones).astype(BF16)
    v_ref[...] = lax.dot_general(wuv_ref[...], ckv, (((1,), (1,)), ((), ())),
                                 preferred_element_type=F32).astype(BF16)


def mla_in(h, pos, inv, gain, w_in, q_gain, kv_gain, wn, wr, wrr, wuk, wuv_t,
           *, batch, heads, q_rank, kv_rank, qk_scale, tm=256):
    t, d = h.shape
    assert (t // batch) % tm == 0
    nsb = t // batch // tm
    row = lambda i: (i, 0)
    kern = functools.partial(_mla_in_kernel, heads=heads, q_rank=q_rank, kv_rank=kv_rank,
                             qk_scale=qk_scale)
    col = jnp.arange(LANE)[None, :]
    e_nope = (jnp.arange(heads * LANE)[:, None] // LANE == col).astype(BF16)
    e_rope = (jnp.arange(heads * LANE // 2)[:, None] // (LANE // 2) == col).astype(BF16)
    ones = jnp.ones((LANE, LANE), BF16)
    consts = [inv, gain, w_in, q_gain, kv_gain, wn, wr, wrr, wuk, wuv_t, e_nope, e_rope, ones]
    return pl.pallas_call(
        kern,
        grid=(t // tm,),
        in_specs=[pl.BlockSpec((tm, d), row), pl.BlockSpec((tm, 1), row)]
                 + [_const_spec(c.shape) for c in consts],
        out_specs=[pl.BlockSpec((tm, 2 * heads * LANE), row),
                   pl.BlockSpec((tm, 2 * heads * LANE), row),
                   pl.BlockSpec((None, heads * LANE, tm), lambda i: (i // nsb, 0, i % nsb))],
        out_shape=[jax.ShapeDtypeStruct((t, 2 * heads * LANE), BF16),
                   jax.ShapeDtypeStruct((t, 2 * heads * LANE), BF16),
                   jax.ShapeDtypeStruct((batch, heads * LANE, t // batch), BF16)],
        compiler_params=_params(dimension_semantics=("arbitrary",)),
        name="mla_in",
    )(h, pos, *consts)


def _attn_stream(q_tiles, k_ref, vt_ref, o_ref, *, tq, tk, ahead, online_max):
    nk = k_ref.shape[0] // tk
    total = len(q_tiles) * nk

    def scores(t):
        i, j = divmod(t, nk)
        return lax.dot_general(k_ref[j * tk:(j + 1) * tk, :], q_tiles[i],
                               (((1,), (1,)), ((), ())),
                               preferred_element_type=F32)

    pending = [scores(t) for t in range(min(ahead, total))]
    for t in range(total):
        i, j = divmod(t, nk)
        s = pending.pop(0)
        if t + ahead < total:
            pending.append(scores(t + ahead))
        if online_max:
            s_max = jnp.max(s, axis=0, keepdims=True)
            m_new = s_max if j == 0 else jnp.maximum(m, s_max)
            s = s - m_new
        p = jnp.exp2(s)
        p_sum = jnp.sum(p, axis=0, keepdims=True)
        pv = jnp.dot(vt_ref[:, j * tk:(j + 1) * tk], p.astype(BF16),
                     preferred_element_type=F32)
        if j == 0:
            l, acc = p_sum, pv
        elif online_max:
            alpha = jnp.exp2(m - m_new)
            l = alpha * l + p_sum
            acc = alpha * acc + pv
        else:
            l = l + p_sum
            acc = acc + pv
        if online_max:
            m = m_new
        if j == nk - 1:
            o_ref[i * tq:(i + 1) * tq, :] = (acc / l).T.astype(o_ref.dtype)


def _attn_kernel(q_ref, k_ref, vt_ref, o_ref, *, tq, tk, ahead):
    nq = q_ref.shape[0] // tq
    lane = lax.broadcasted_iota(jnp.int32, (1, LANE), 1)
    bound_lane = lane == Q_BOUND_LANE

    def over_rows(x, op):
        while x.shape[0] > 16:
            half = x.shape[0] // 2
            x = op(x[:half], x[half:])
        return x.astype(F32)

    k_top = jnp.max(over_rows(k_ref[:, LANE:2 * LANE], jnp.maximum), axis=0, keepdims=True)
    k_max = jnp.max(jnp.where(lane == K_NORM_LANE, k_top, 0.0), axis=1, keepdims=True)
    q_low = jnp.min(over_rows(q_ref[:, LANE:2 * LANE], jnp.minimum), axis=0, keepdims=True)
    q_max = jnp.max(jnp.where(bound_lane, -q_low, 0.0), axis=1, keepdims=True)
    worst = jnp.max(q_max * k_max)
    lane_scale = jnp.where(bound_lane, k_max, 1.0).astype(BF16)
    kw = dict(tq=tq, tk=tk, ahead=ahead)

    def q_tiles(fix_rope):
        return [jnp.concatenate([q_ref[i * tq:(i + 1) * tq, :LANE],
                                 fix_rope(q_ref[i * tq:(i + 1) * tq, LANE:2 * LANE])], axis=1)
                for i in range(nq)]

    @pl.when(worst <= FAST_SOFTMAX_MAX_BOUND)
    def _():
        tiles = q_tiles(lambda r: r * lane_scale)
        _attn_stream(tiles, k_ref, vt_ref, o_ref, online_max=False, **kw)

    @pl.when(jnp.logical_not(worst <= FAST_SOFTMAX_MAX_BOUND))
    def _():
        tiles = q_tiles(lambda r: jnp.where(bound_lane, jnp.zeros((), BF16), r))
        _attn_stream(tiles, k_ref, vt_ref, o_ref, online_max=True, **kw)


def attention(q, k, vt, *, batch, seq, heads, tq=512, nq=8, tk=1024, ahead=2):
    dq = q.shape[-1] // heads
    dv = vt.shape[1] // heads
    tqb = tq * nq
    assert seq % tqb == 0 and seq % tk == 0
    return pl.pallas_call(
        functools.partial(_attn_kernel, tq=tq, tk=tk, ahead=ahead),
        grid=(batch, heads, seq // tqb),
        in_specs=[pl.BlockSpec((None, tqb, dq), lambda b, h, i: (b, i, h)),
                  pl.BlockSpec((None, seq, dq), lambda b, h, i: (b, 0, h)),
                  pl.BlockSpec((None, dv, seq), lambda b, h, i: (b, h, 0))],
        out_specs=pl.BlockSpec((None, tqb, dv), lambda b, h, i: (b, i, h)),
        out_shape=jax.ShapeDtypeStruct((batch, seq, heads * dv), BF16),
        compiler_params=_params(dimension_semantics=("arbitrary",) * 3),
        name="attention",
    )(q, k, vt)


def _rot_half_cols(w):
    half = w.shape[-1] // 2
    return jnp.concatenate([-w[..., half:], w[..., :half]], axis=-1)


def kernel(x, p, positions, pre_mix_norm, post_mix_norm, pre_mlp_norm, post_mlp_norm,
           w_mlp_in, w_mlp_out, w_ple_proj, w_ple_gate, ple_norm,
           hg_lb_logits, hg_w_in, hg_o_norm, hg_w_out,
           mla_w_in, mla_q_norm, mla_w_uq, mla_kv_norm, mla_w_ukv, mla_w_o):
    batch, seq, d = x.shape
    depth = p.shape[0]
    t = batch * seq
    hg_heads, hg_vdim = hg_o_norm.shape[1], hg_o_norm.shape[2]
    hg_kdim = hg_lb_logits.shape[1] // hg_heads
    q_rank = mla_q_norm.shape[1]
    kv_rank = mla_kv_norm.shape[1]
    rope = mla_w_in.shape[2] - q_rank - kv_rank
    mla_heads = mla_w_o.shape[1] // LANE
    nope = mla_w_uq.shape[2] // mla_heads - rope
    vdim = mla_w_ukv.shape[2] // mla_heads - nope
    assert nope == LANE and vdim == LANE and 2 * rope == LANE

    row = lambda g: g.reshape(1, -1).astype(F32)
    h = x.reshape(t, d)
    pos = positions.reshape(t, 1)
    inv = 1.0 / (ROPE_BASE ** (jnp.arange(0, rope, 2, dtype=F32) / rope))
    inv = jnp.tile(inv, LANE // inv.shape[0]).reshape(1, LANE)
    qk_scale = (nope + rope) ** -0.5 * math.log2(math.e)

    for i in range(depth):
        j = i // 2
        if i % 2 == 0:
            proj = norm_matmul(h, row(pre_mix_norm[i]), hg_w_in[j].astype(BF16))
            o_f, o_b = gla_bidirectional(
                proj.reshape(batch, seq, -1), hg_lb_logits.astype(F32), layer=i,
                batch=batch, seq=seq, heads=hg_heads, kdim=hg_kdim)
            hv = hg_heads * hg_vdim
            tile = lambda col: pl.BlockSpec((TAIL_ROWS, hv), lambda r: (r, col))
            mixer_in = (o_f.reshape(t, -1), o_b.reshape(t, -1), proj, row(hg_o_norm[j]))
            mixer_specs = (tile(0), tile(0), tile(4), _const_spec((1, hv)))
            w_mix, tail_kw = hg_w_out[j], dict(hgrn_heads=hg_heads, hgrn_vdim=hg_vdim)
        else:
            w_in = mla_w_in[j]
            c0 = q_rank + kv_rank
            w_kr = w_in[:, c0:]
            w_krr = _rot_half_cols(w_kr)
            w_in_ext = jnp.concatenate([w_in[:, :c0], w_kr, w_kr, w_krr, w_krr],
                                       axis=-1).astype(BF16)
            w_uq = mla_w_uq[j].reshape(q_rank, mla_heads, nope + rope)
            wn = w_uq[..., :nope].reshape(q_rank, -1).astype(BF16)
            w_qr = w_uq[..., nope:]
            wr = w_qr.reshape(q_rank, -1).astype(BF16)
            wrr = _rot_half_cols(w_qr).reshape(q_rank, -1).astype(BF16)
            w_ukv = mla_w_ukv[j].reshape(kv_rank, mla_heads, nope + vdim)
            wuk = w_ukv[..., :nope].reshape(kv_rank, -1).astype(BF16)
            wuv_t = w_ukv[..., nope:].reshape(kv_rank, -1).T.astype(BF16)
            q, k, vt = mla_in(h, pos, inv, row(pre_mix_norm[i]), w_in_ext,
                              row(mla_q_norm[j]), row(mla_kv_norm[j]), wn, wr, wrr, wuk, wuv_t,
                              batch=batch, heads=mla_heads, q_rank=q_rank, kv_rank=kv_rank,
                              qk_scale=qk_scale)
            o = attention(q.reshape(batch, seq, -1), k.reshape(batch, seq, -1), vt,
                          batch=batch, seq=seq, heads=mla_heads)
            o = o.reshape(t, -1)
            mixer_in = (o,)
            mixer_specs = (pl.BlockSpec((TAIL_ROWS, o.shape[1]), lambda r: (r, 0)),)
            w_mix, tail_kw = mla_w_o[j], {}
        consts = (w_mix.astype(BF16), row(post_mix_norm[i]), row(pre_mlp_norm[i]),
                  w_mlp_in.astype(BF16), w_mlp_out.astype(BF16),
                  row(post_mlp_norm[i]), w_ple_proj.astype(BF16),
                  w_ple_gate.astype(BF16), row(ple_norm[i]))
        h = layer_tail(mixer_in, mixer_specs, h, p.reshape(depth, t, -1), consts,
                       layer=i, tm=TAIL_ROWS, **tail_kw)
    return h.reshape(batch, seq, d)
```
